```python
import math
import jax, jax.numpy as jnp
from jax import lax
import numpy as np

D_MODEL = 4096
BATCH = 2
SEQ = 8192
DEPTH = 1
DEC_BATCH = 16
DEC_SEQ = 64
PAST_LEN = 2048

CHUNK = 64
PLE_DIM = 256
MIX_W = D_MODEL
POOL_W = MIX_W // 2
SSM_W = MIX_W - POOL_W
POOL_WINDOWS = (2, 4, 8, 16)
POOL_GROUPS = len(POOL_WINDOWS)
POOL_GROUP = POOL_W // POOL_GROUPS
POOL_BUF = max(POOL_WINDOWS) - 1
SSM_CH = 16
SSM_GROUPS = SSM_W // SSM_CH
SSM_N = 64
PEER_HEADS = 8
PEER_NKEYS = 128
PEER_EXPERTS = PEER_NKEYS * PEER_NKEYS
PEER_DK = 256
PEER_DK_HALF = PEER_DK // 2
PEER_TOPK = 16
PEER_TOK_BLOCK = 128
EPS = 1e-6

kernel_name = 'hybrid_pool_s5_peer_stream_step'


def rmsnorm(x, g):
    xf = x.astype(jnp.float32)
    y = xf * lax.rsqrt(jnp.mean(xf * xf, axis=-1, keepdims=True) + EPS) * g.astype(jnp.float32)
    return y.astype(x.dtype)


def pool_mixer(z, buf, pos0, w_pool, scale):
    bsz, L, C = z.shape
    zc = jnp.concatenate([buf.astype(z.dtype), z], axis=1)
    zf = zc.astype(jnp.float32)
    cs = jnp.concatenate([jnp.zeros((bsz, 1, C), jnp.float32), jnp.cumsum(zf, axis=1)], axis=1)
    pos = pos0 + jnp.arange(L)
    means = []
    for g, w in enumerate(POOL_WINDOWS):
        sl = slice(g * POOL_GROUP, (g + 1) * POOL_GROUP)
        s = cs[:, POOL_BUF + 1:POOL_BUF + 1 + L, sl] - cs[:, POOL_BUF + 1 - w:POOL_BUF + 1 - w + L, sl]
        cnt = jnp.minimum(pos + 1, w).astype(jnp.float32)[None, :, None]
        means.append(s / cnt)
    d = jnp.concatenate(means, axis=-1) - zf[:, POOL_BUF:]
    d = d.reshape(bsz, L, POOL_GROUPS, POOL_GROUP)
    y = jnp.einsum('blgc,gcd->blgd', d, w_pool.astype(jnp.float32)).reshape(bsz, L, POOL_W)
    y = y * scale.astype(jnp.float32)
    return y.astype(z.dtype), zc[:, -POOL_BUF:]


def ssm_mixer(u, h_re, h_im, a_re, a_im, log_dt, b_re, b_im, c_re, c_im, d_skip):
    bsz, L, _ = u.shape
    f32 = jnp.float32
    uf = u.astype(f32).reshape(bsz, L, SSM_GROUPS, SSM_CH)
    lam = lax.complex(a_re.astype(f32), a_im.astype(f32))
    dt = jnp.exp(log_dt.astype(f32))[:, None]
    lam_bar = jnp.exp(lam * dt)
    b_bar = ((lam_bar - 1.0) / lam)[..., None] * lax.complex(b_re.astype(f32), b_im.astype(f32))
    c_mat = lax.complex(c_re.astype(f32), c_im.astype(f32))
    dsk = d_skip.astype(f32)
    h0 = lax.complex(h_re.astype(f32), h_im.astype(f32))
    blk = CHUNK if L % CHUNK == 0 else L
    nb = L // blk
    ub = uf.reshape(bsz, nb, blk, SSM_GROUPS, SSM_CH).transpose(1, 0, 2, 3, 4)

    def combine(e1, e2):
        a1, s1 = e1
        a2, s2 = e2
        return a1 * a2, a2 * s1 + s2

    def step(h, u_blk):
        bu = jnp.einsum('blgi,gni->blgn', u_blk.astype(jnp.complex64), b_bar)
        bu = bu.at[:, 0].add(lam_bar[None] * h)
        a = jnp.broadcast_to(lam_bar, bu.shape)
        _, hs = lax.associative_scan(combine, (a, bu), axis=1)
        y = jnp.einsum('gin,blgn->blgi', c_mat, hs).real + dsk[None, None] * u_blk
        return hs[:, -1], y

    h_last, ys = lax.scan(step, h0, ub)
    y = ys.transpose(1, 0, 2, 3, 4).reshape(bsz, L, SSM_W)
    return y, jnp.real(h_last), jnp.imag(h_last)


def peer(xn, w_query, sub_keys, expert_u, expert_v):
    bsz, L, D = xn.shape
    T = bsz * L
    xf = xn.reshape(T, D)
    q = (xf @ w_query).astype(jnp.float32).reshape(T, PEER_HEADS, 2, PEER_DK_HALF)
    s = jnp.einsum('thpk,hpnk->thpn', q, sub_keys.astype(jnp.float32))
    s1, i1 = lax.top_k(s[:, :, 0], PEER_TOPK)
    s2, i2 = lax.top_k(s[:, :, 1], PEER_TOPK)
    cand = (s1[..., :, None] + s2[..., None, :]).reshape(T, PEER_HEADS, PEER_TOPK * PEER_TOPK)
    cidx = (i1[..., :, None] * PEER_NKEYS + i2[..., None, :]).reshape(T, PEER_HEADS, PEER_TOPK * PEER_TOPK)
    top, sel = lax.top_k(cand, PEER_TOPK)
    idx = jnp.take_along_axis(cidx, sel, axis=-1)
    gates = jax.nn.softmax(top, axis=-1)
    nblk = -(-T // PEER_TOK_BLOCK)
    pad = nblk * PEER_TOK_BLOCK - T
    xr = jnp.pad(xf, ((0, pad), (0, 0))).reshape(nblk, PEER_TOK_BLOCK, D)
    ir = jnp.pad(idx, ((0, pad), (0, 0), (0, 0))).reshape(nblk, PEER_TOK_BLOCK, PEER_HEADS, PEER_TOPK)
    gr = jnp.pad(gates, ((0, pad), (0, 0), (0, 0))).reshape(nblk, PEER_TOK_BLOCK, PEER_HEADS, PEER_TOPK)

    def block(args):
        xb, ib, gb = args
        u = expert_u[ib]
        hpre = jnp.einsum('td,thkd->thk', xb, u).astype(jnp.float32)
        act = (jax.nn.gelu(hpre, approximate=False) * gb).astype(xb.dtype)
        v = expert_v[ib]
        return jnp.einsum('thk,thkd->td', act, v)

    out = lax.map(block, (xr, ir, gr)).reshape(nblk * PEER_TOK_BLOCK, D)[:T]
    return out.reshape(bsz, L, D).astype(xn.dtype)


def layer(x, p, pool_buf, h_re, h_im, pos0, g_mix, w_in, w_pool, pool_scale, a_re, a_im, log_dt,
          b_re, b_im, c_re, c_im, d_skip, w_glu, b_glu, w_out, g_ffn, w_query, sub_keys,
          expert_u, expert_v, g_ple, w_ple_gate, w_ple):
    h = rmsnorm(x, g_mix)
    z = h @ w_in
    y_pool, new_buf = pool_mixer(z[..., :POOL_W], pool_buf, pos0, w_pool, pool_scale)
    y_ssm, new_re, new_im = ssm_mixer(z[..., POOL_W:], h_re, h_im, a_re, a_im, log_dt,
                                      b_re, b_im, c_re, c_im, d_skip)
    s = jax.nn.gelu(y_ssm, approximate=False)
    s = s * jax.nn.sigmoid(s @ w_glu.astype(jnp.float32) + b_glu.astype(jnp.float32))
    mixed = jnp.concatenate([y_pool, s.astype(y_pool.dtype)], axis=-1)
    x = x + (mixed @ w_out).astype(x.dtype)
    x = x + peer(rmsnorm(x, g_ffn), w_query, sub_keys, expert_u, expert_v)
    gate = jax.nn.sigmoid((rmsnorm(x, g_ple) @ w_ple_gate).astype(jnp.float32))
    x = x + ((p.astype(jnp.float32) @ w_ple.astype(jnp.float32)) * gate).astype(x.dtype)
    return x, new_buf, new_re, new_im


def setup_inputs(seed: int = 0) -> dict:
    key = jax.random.key(seed)
    ks = jax.random.split(key, 32)
    f32 = jnp.float32

    def nrm(k, shape, scale):
        return scale * jax.random.normal(k, shape, f32)

    L = DEPTH
    G, N = SSM_GROUPS, SSM_N
    return {
        'x_prompt': nrm(ks[0], (BATCH, SEQ, D_MODEL), 1.0),
        'x_sample': nrm(ks[1], (DEC_BATCH, DEC_SEQ, D_MODEL), 1.0),
        'p_prompt': nrm(ks[2], (L, BATCH, SEQ, PLE_DIM), 1.0),
        'p_sample': nrm(ks[3], (L, DEC_BATCH, DEC_SEQ, PLE_DIM), 1.0),
        'cache_pool': nrm(ks[4], (L, DEC_BATCH, POOL_BUF, POOL_W), 1.0),
        'state_ssm_re': nrm(ks[5], (L, DEC_BATCH, G, N), 0.5),
        'state_ssm_im': nrm(ks[6], (L, DEC_BATCH, G, N), 0.5),
        'g_mix': 1.0 + nrm(ks[7], (L, D_MODEL), 0.02),
        'w_in': nrm(ks[8], (L, D_MODEL, MIX_W), D_MODEL ** -0.5),
        'w_pool': nrm(ks[9], (L, POOL_GROUPS, POOL_GROUP, POOL_GROUP), POOL_GROUP ** -0.5),
        'pool_scale': 1.0 + nrm(ks[10], (L, POOL_W), 0.02),
        'ssm_a_re': -0.5 + nrm(ks[11], (L, G, N), 0.01),
        'ssm_a_im': math.pi * jnp.arange(N, dtype=f32) + nrm(ks[12], (L, G, N), 0.01),
        'ssm_log_dt': jax.random.uniform(ks[13], (L, G), f32, math.log(1e-3), math.log(1e-1)),
        'ssm_b_re': nrm(ks[14], (L, G, N, SSM_CH), (2 * SSM_CH) ** -0.5),
        'ssm_b_im': nrm(ks[15], (L, G, N, SSM_CH), (2 * SSM_CH) ** -0.5),
        'ssm_c_re': nrm(ks[16], (L, G, SSM_CH, N), N ** -0.5),
        'ssm_c_im': nrm(ks[17], (L, G, SSM_CH, N), N ** -0.5),
        'ssm_d': nrm(ks[18], (L, G, SSM_CH), 1.0),
        'w_glu': nrm(ks[19], (L, SSM_W, SSM_W), SSM_W ** -0.5),
        'b_glu': nrm(ks[20], (L, SSM_W), 0.02),
        'w_out': nrm(ks[21], (L, MIX_W, D_MODEL), MIX_W ** -0.5),
        'g_ffn': 1.0 + nrm(ks[22], (L, D_MODEL), 0.02),
        'w_query': nrm(ks[23], (L, D_MODEL, PEER_HEADS * PEER_DK), D_MODEL ** -0.5),
        'peer_sub_keys': nrm(ks[24], (L, PEER_HEADS, 2, PEER_NKEYS, PEER_DK_HALF), PEER_DK_HALF ** -0.5),
        'expert_u': nrm(ks[25], (L, PEER_EXPERTS, D_MODEL), D_MODEL ** -0.5),
        'expert_v': nrm(ks[26], (L, PEER_EXPERTS, D_MODEL), PEER_HEADS ** -0.5),
        'g_ple': 1.0 + nrm(ks[27], (L, D_MODEL), 0.02),
        'w_ple_gate': nrm(ks[28], (L, D_MODEL, D_MODEL), D_MODEL ** -0.5),
        'w_ple': nrm(ks[29], (L, PLE_DIM, D_MODEL), PLE_DIM ** -0.5),
        'g_final': 1.0 + nrm(ks[30], (D_MODEL,), 0.02),
    }


def reference(x_prompt, x_sample, p_prompt, p_sample, cache_pool, state_ssm_re, state_ssm_im,
              g_mix, w_in, w_pool, pool_scale, ssm_a_re, ssm_a_im, ssm_log_dt, ssm_b_re, ssm_b_im,
              ssm_c_re, ssm_c_im, ssm_d, w_glu, b_glu, w_out, g_ffn, w_query, peer_sub_keys,
              expert_u, expert_v, g_ple, w_ple_gate, w_ple, g_final):
    bp = x_prompt.shape[0]
    xp = x_prompt
    xs = x_sample
    pool_p, re_p, im_p, pool_s, re_s, im_s = [], [], [], [], [], []
    for i in range(DEPTH):
        lw = (g_mix[i], w_in[i], w_pool[i], pool_scale[i], ssm_a_re[i], ssm_a_im[i], ssm_log_dt[i],
              ssm_b_re[i], ssm_b_im[i], ssm_c_re[i], ssm_c_im[i], ssm_d[i], w_glu[i], b_glu[i],
              w_out[i], g_ffn[i], w_query[i], peer_sub_keys[i], expert_u[i], expert_v[i],
              g_ple[i], w_ple_gate[i], w_ple[i])
        buf0 = jnp.zeros((bp, POOL_BUF, POOL_W), xp.dtype)
        h0 = jnp.zeros((bp, SSM_GROUPS, SSM_N), jnp.float32)
        xp, nb_p, nr_p, ni_p = layer(xp, p_prompt[i], buf0, h0, h0, 0, *lw)
        xs, nb_s, nr_s, ni_s = layer(xs, p_sample[i], cache_pool[i], state_ssm_re[i], state_ssm_im[i],
                                     PAST_LEN, *lw)
        pool_p.append(nb_p)
        re_p.append(nr_p)
        im_p.append(ni_p)
        pool_s.append(nb_s)
        re_s.append(nr_s)
        im_s.append(ni_s)
    y_prompt = rmsnorm(xp, g_final)
    y_sample = rmsnorm(xs, g_final)
    new_pool_prompt = jnp.stack(pool_p)
    new_ssm_re_prompt = jnp.stack(re_p)
    new_ssm_im_prompt = jnp.stack(im_p)
    new_pool_sample = jnp.stack(pool_s)
    new_ssm_re_sample = jnp.stack(re_s)
    new_ssm_im_sample = jnp.stack(im_s)
    return (y_prompt, y_sample, new_pool_prompt, new_ssm_re_prompt, new_ssm_im_prompt,
            new_pool_sample, new_ssm_re_sample, new_ssm_im_sample)
```

```python
import functools
import math

import jax
import jax.numpy as jnp
from jax import lax
from jax.experimental import pallas as pl
from jax.experimental.pallas import tpu as pltpu

F32 = jnp.float32
BF16 = jnp.bfloat16
EPS = 1e-6

SUBLANES = 8
LANES = 128
VMEM_LIMIT_BYTES = 56 * 1024 * 1024

POOL_WINDOWS = (2, 4, 8, 16)
POOL_HALO = 16
SSM_CH = 16
SSM_N = 64
SSM_BLOCK = 8
SSM_SLAB_GROUPS = 16
PEER_HEADS = 8
PEER_NKEYS = 128
PEER_DK_HALF = 128
PEER_TOPK = 16


def _cparams(*sem):
    return pltpu.CompilerParams(dimension_semantics=sem, vmem_limit_bytes=VMEM_LIMIT_BYTES)


def _rmsnorm(x, g):
    ms = jnp.mean(x * x, axis=-1, keepdims=True)
    return x * lax.rsqrt(ms + EPS) * g


def _gelu(x):
    return 0.5 * x * (1.0 + lax.erf(x * (1.0 / math.sqrt(2.0))))


def _norm_matmul_kernel(x_ref, g_ref, w_ref, o_ref, *rest, emit_xn):
    if emit_xn:
        xn_out_ref, xn_ref = rest
    else:
        (xn_ref,) = rest

    @pl.when(pl.program_id(1) == 0)
    def _():
        xn = _rmsnorm(x_ref[...], g_ref[...]).astype(BF16)
        xn_ref[...] = xn
        if emit_xn:
            xn_out_ref[...] = xn

    o_ref[...] = jnp.dot(xn_ref[...], w_ref[...], preferred_element_type=F32)


def _norm_matmul(x, g, w, *, bm, bn, emit_xn, name):
    t, d = x.shape
    n = w.shape[1]
    out_shape = [jax.ShapeDtypeStruct((t, n), F32)]
    out_specs = [pl.BlockSpec((bm, bn), lambda i, j: (i, j))]
    if emit_xn:
        out_shape.append(jax.ShapeDtypeStruct((t, d), BF16))
        out_specs.append(pl.BlockSpec((bm, d), lambda i, j: (i, 0)))
    res = pl.pallas_call(
        functools.partial(_norm_matmul_kernel, emit_xn=emit_xn),
        grid=(t // bm, n // bn),
        in_specs=[
            pl.BlockSpec((bm, d), lambda i, j: (i, 0)),
            pl.BlockSpec((1, d), lambda i, j: (0, 0)),
            pl.BlockSpec((d, bn), lambda i, j: (0, j)),
        ],
        out_specs=out_specs,
        out_shape=out_shape,
        scratch_shapes=[pltpu.VMEM((bm, d), BF16)],
        compiler_params=_cparams("parallel", "arbitrary"),
        name=name,
    )(x, g, w)
    return res if emit_xn else res[0]


def _pool_kernel(z_ref, init_ref, w_ref, sc_ref, o_ref, e_ref, *, tt, pos0, pg):
    i = pl.program_id(1)

    @pl.when(i == 0)
    def _():
        e_ref[0:POOL_HALO, :] = init_ref[0]

    @pl.when(i > 0)
    def _():
        e_ref[0:POOL_HALO, :] = e_ref[tt:tt + POOL_HALO, :]

    e_ref[POOL_HALO:POOL_HALO + tt, :] = z_ref[...]
    pos1 = lax.broadcasted_iota(jnp.int32, (tt, 1), 0) + (i * tt + pos0 + 1)
    for g, w in enumerate(POOL_WINDOWS):
        cols = slice(g * pg, (g + 1) * pg)
        s = e_ref[POOL_HALO:POOL_HALO + tt, cols]
        for back in range(1, w):
            s = s + e_ref[POOL_HALO - back:POOL_HALO - back + tt, cols]
        cnt = jnp.minimum(pos1, w).astype(F32)
        d = s / cnt - z_ref[:, cols]
        y = jnp.dot(d.astype(BF16), w_ref[g], preferred_element_type=F32)
        o_ref[:, cols] = y * sc_ref[:, cols]


def _pool_mixer(z, init, w_pool, scale, *, nseq, seq_len, tt, pos0, name):
    t = z.shape[0]
    pool_w = init.shape[-1]
    pg = pool_w // len(POOL_WINDOWS)
    nt = seq_len // tt
    return pl.pallas_call(
        functools.partial(_pool_kernel, tt=tt, pos0=pos0, pg=pg),
        grid=(nseq, nt),
        in_specs=[
            pl.BlockSpec((tt, pool_w), lambda b, i: (b * nt + i, 0)),
            pl.BlockSpec((1, POOL_HALO, pool_w), lambda b, i: (b, 0, 0)),
            pl.BlockSpec((len(POOL_WINDOWS), pg, pg), lambda b, i: (0, 0, 0)),
            pl.BlockSpec((1, pool_w), lambda b, i: (0, 0)),
        ],
        out_specs=pl.BlockSpec((tt, pool_w), lambda b, i: (b * nt + i, 0)),
        out_shape=jax.ShapeDtypeStruct((t, pool_w), F32),
        scratch_shapes=[pltpu.VMEM((tt + POOL_HALO, pool_w), F32)],
        compiler_params=_cparams("parallel", "arbitrary"),
        name=name,
    )(z, init, w_pool, scale)


def _ssm_param_kernel(are_ref, aim_ref, ldt_ref, bre_ref, bim_ref, pre_ref, pim_ref, bbre_ref, bbim_ref):
    a_re = are_ref[...]
    a_im = aim_ref[...]
    dt = jnp.exp(ldt_ref[...])
    ar = a_re * dt
    ai = a_im * dt
    for p in range(SSM_BLOCK + 1):
        mag = jnp.exp(ar * float(p))
        pre_ref[p] = mag * jnp.cos(ai * float(p))
        pim_ref[p] = mag * jnp.sin(ai * float(p))
    x = pre_ref[1] - 1.0
    y = pim_ref[1]
    den = a_re * a_re + a_im * a_im
    cr = (x * a_re + y * a_im) / den
    ci = (y * a_re - x * a_im) / den
    for c in range(SSM_CH):
        bbre_ref[c] = cr * bre_ref[c] - ci * bim_ref[c]
        bbim_ref[c] = cr * bim_ref[c] + ci * bre_ref[c]


def _ssm_params(a_re, a_im, log_dt, b_re, b_im):
    g, n = a_re.shape
    npow = SSM_BLOCK + 1
    full = lambda *shape: pl.BlockSpec(shape, lambda: (0,) * len(shape))
    return pl.pallas_call(
        _ssm_param_kernel,
        in_specs=[full(g, n), full(g, n), full(g, 1), full(SSM_CH, g, n), full(SSM_CH, g, n)],
        out_specs=[full(npow, g, n), full(npow, g, n), full(SSM_CH, g, n), full(SSM_CH, g, n)],
        out_shape=[
            jax.ShapeDtypeStruct((npow, g, n), F32),
            jax.ShapeDtypeStruct((npow, g, n), F32),
            jax.ShapeDtypeStruct((SSM_CH, g, n), F32),
            jax.ShapeDtypeStruct((SSM_CH, g, n), F32),
        ],
        name="ssm_params",
    )(a_re, a_im, log_dt.reshape(g, 1), jnp.transpose(b_re, (2, 0, 1)), jnp.transpose(b_im, (2, 0, 1)))


def _block_diag(x):
    s, g, r, c = x.shape
    eye = jnp.eye(g, dtype=x.dtype)
    return jnp.einsum("sgrc,gh->sgrhc", x, eye).reshape(s, g * r, g * c)


def _ssm_kernel(z_ref, b_ref, c_ref, pw_ref, d_ref, h0_ref, o_ref, hl_ref, xs_ref, hin_ref, h_ref,
                *, nseq, nb, half, chunk):
    i = pl.program_id(2)
    rows = nb * nseq

    @pl.when(i == 0)
    def _():
        h_ref[...] = h0_ref[0, 0]

    u = z_ref[...]
    xs_ref[...] = jnp.dot(u.astype(BF16), b_ref[0], preferred_element_type=F32)

    def lane_chunks():
        for k in range(half // chunk):
            yield slice(k * chunk, (k + 1) * chunk), slice(half + k * chunk, half + (k + 1) * chunk)

    for l in range(1, SSM_BLOCK):
        prev = slice((l - 1) * rows, l * rows)
        cur = slice(l * rows, (l + 1) * rows)
        for re, im in lane_chunks():
            lr = pw_ref[0, 1:2, re]
            li = pw_ref[0, 1:2, im]
            pr = xs_ref[prev, re]
            pi = xs_ref[prev, im]
            xs_ref[cur, re] += pr * lr - pi * li
            xs_ref[cur, im] += pr * li + pi * lr

    l8r = pw_ref[0, SSM_BLOCK:SSM_BLOCK + 1, 0:half]
    l8i = pw_ref[0, SSM_BLOCK:SSM_BLOCK + 1, half:2 * half]
    last = (SSM_BLOCK - 1) * rows

    def step(c, carry):
        hr, hi = carry
        r0 = c * nseq
        hin_ref[pl.ds(r0, nseq), 0:half] = hr
        hin_ref[pl.ds(r0, nseq), half:2 * half] = hi
        sr = xs_ref[pl.ds(last + r0, nseq), 0:half]
        si = xs_ref[pl.ds(last + r0, nseq), half:2 * half]
        return l8r * hr - l8i * hi + sr, l8r * hi + l8i * hr + si

    hr, hi = lax.fori_loop(0, nb, step, (h_ref[:, 0:half], h_ref[:, half:2 * half]))
    h_ref[:, 0:half] = hr
    h_ref[:, half:2 * half] = hi

    for l in range(SSM_BLOCK):
        cur = slice(l * rows, (l + 1) * rows)
        for re, im in lane_chunks():
            lr = pw_ref[0, l + 1:l + 2, re]
            li = pw_ref[0, l + 1:l + 2, im]
            pr = hin_ref[:, re]
            pi = hin_ref[:, im]
            xs_ref[cur, re] += pr * lr - pi * li
            xs_ref[cur, im] += pr * li + pi * lr

    y = jnp.dot(xs_ref[...].astype(BF16), c_ref[0], preferred_element_type=F32) + d_ref[0] * u
    o_ref[...] = _gelu(y)

    @pl.when(i == pl.num_programs(2) - 1)
    def _():
        hl_ref[0, 0] = h_ref[...]


def _ssm_mixer(zp, bbd, cbd, pows, dsk, h0, *, ngroups, nseq, nb, name):
    t, ssm_w = zp.shape
    slabs = bbd.shape[0]
    sw = ssm_w // slabs
    half = bbd.shape[2] // 2
    tt = SSM_BLOCK * nb * nseq
    nt = t // (ngroups * tt)
    npow = pows.shape[1]
    kernel = functools.partial(_ssm_kernel, nseq=nseq, nb=nb, half=half, chunk=min(half, 2 * LANES))
    return pl.pallas_call(
        kernel,
        grid=(slabs, ngroups, nt),
        in_specs=[
            pl.BlockSpec((tt, sw), lambda s, b, i: (b * nt + i, s)),
            pl.BlockSpec((1, sw, 2 * half), lambda s, b, i: (s, 0, 0)),
            pl.BlockSpec((1, 2 * half, sw), lambda s, b, i: (s, 0, 0)),
            pl.BlockSpec((1, npow, 2 * half), lambda s, b, i: (s, 0, 0)),
            pl.BlockSpec((1, 1, sw), lambda s, b, i: (s, 0, 0)),
            pl.BlockSpec((1, 1, nseq, 2 * half), lambda s, b, i: (s, b, 0, 0)),
        ],
        out_specs=[
            pl.BlockSpec((tt, sw), lambda s, b, i: (b * nt + i, s)),
            pl.BlockSpec((1, 1, nseq, 2 * half), lambda s, b, i: (s, b, 0, 0)),
        ],
        out_shape=[
            jax.ShapeDtypeStruct((t, ssm_w), F32),
            jax.ShapeDtypeStruct((slabs, ngroups, nseq, 2 * half), F32),
        ],
        scratch_shapes=[
            pltpu.VMEM((tt, 2 * half), F32),
            pltpu.VMEM((nb * nseq, 2 * half), F32),
            pltpu.VMEM((nseq, 2 * half), F32),
        ],
        compiler_params=_cparams("parallel", "arbitrary", "arbitrary"),
        name=name,
    )(zp, bbd, cbd, pows, dsk, h0)


def _glu_kernel(s_ref, w_ref, b_ref, o_ref, sb_ref, *, bn):
    j = pl.program_id(1)

    @pl.when(j == 0)
    def _():
        sb_ref[...] = s_ref[...].astype(BF16)

    acc = jnp.dot(sb_ref[...], w_ref[...], preferred_element_type=F32) + b_ref[...]
    col = pl.multiple_of(j * bn, bn)
    o_ref[...] = s_ref[:, pl.ds(col, bn)] * jax.nn.sigmoid(acc)


def _glu(s, w, b, *, bm, bn, name):
    t, k = s.shape
    n = w.shape[1]
    return pl.pallas_call(
        functools.partial(_glu_kernel, bn=bn),
        grid=(t // bm, n // bn),
        in_specs=[
            pl.BlockSpec((bm, k), lambda i, j: (i, 0)),
            pl.BlockSpec((k, bn), lambda i, j: (0, j)),
            pl.BlockSpec((1, bn), lambda i, j: (0, j)),
        ],
        out_specs=pl.BlockSpec((bm, bn), lambda i, j: (i, j)),
        out_shape=jax.ShapeDtypeStruct((t, n), F32),
        scratch_shapes=[pltpu.VMEM((bm, k), BF16)],
        compiler_params=_cparams("parallel", "arbitrary"),
        name=name,
    )(s, w, b)


def _outproj_kernel(a_ref, b_ref, x_ref, w_ref, o_ref, lhs_ref, *, ka):
    @pl.when(pl.program_id(1) == 0)
    def _():
        lhs_ref[:, 0:ka] = a_ref[...].astype(BF16)
        lhs_ref[:, ka:] = b_ref[...].astype(BF16)

    o_ref[...] = x_ref[...] + jnp.dot(lhs_ref[...], w_ref[...], preferred_element_type=F32)


def _outproj(a, b, x, w, *, bm, bn, name):
    t, ka = a.shape
    kb = b.shape[1]
    n = w.shape[1]
    return pl.pallas_call(
        functools.partial(_outproj_kernel, ka=ka),
        grid=(t // bm, n // bn),
        in_specs=[
            pl.BlockSpec((bm, ka), lambda i, j: (i, 0)),
            pl.BlockSpec((bm, kb), lambda i, j: (i, 0)),
            pl.BlockSpec((bm, bn), lambda i, j: (i, j)),
            pl.BlockSpec((ka + kb, bn), lambda i, j: (0, j)),
        ],
        out_specs=pl.BlockSpec((bm, bn), lambda i, j: (i, j)),
        out_shape=jax.ShapeDtypeStruct((t, n), F32),
        scratch_shapes=[pltpu.VMEM((bm, ka + kb), BF16)],
        compiler_params=_cparams("parallel", "arbitrary"),
        name=name,
    )(a, b, x, w)


def _top_rows(x, k):
    tops = []
    for _ in range(k):
        m = jnp.max(x, axis=0, keepdims=True)
        tops.append(m)
        x = jnp.where(x == m, -jnp.inf, x)
    return tops


def _route_kernel(q_ref, k_ref, r_ref, thr_ref, *, tt):
    for h in range(PEER_HEADS):
        scores, tops = [], []
        for p in range(2):
            c0 = (2 * h + p) * PEER_DK_HALF
            s = lax.dot_general(k_ref[h, p], q_ref[:, c0:c0 + PEER_DK_HALF], (((1,), (1,)), ((), ())),
                                precision=lax.Precision.HIGHEST, preferred_element_type=F32)
            scores.append(s)
            tops.append(_top_rows(s, PEER_TOPK))
        v2 = jnp.concatenate(tops[1], axis=0)
        cand = jnp.concatenate([t1 + v2 for t1 in tops[0]], axis=0)
        best = _top_rows(cand, PEER_TOPK)
        zsum = jnp.zeros_like(best[0])
        for b in best:
            zsum = zsum + jnp.exp(b - best[0])
        r_ref[h, 0] = scores[0]
        r_ref[h, 1] = scores[1]
        r_ref[h, 2] = jnp.exp(scores[0] - tops[0][0]) / zsum
        r_ref[h, 3] = jnp.exp(scores[1] - tops[1][0])
        thr_ref[h] = jnp.broadcast_to(best[PEER_TOPK - 1], (SUBLANES, tt))


def _route(q, keys, *, tt, name):
    t, dq = q.shape
    return pl.pallas_call(
        functools.partial(_route_kernel, tt=tt),
        grid=(t // tt,),
        in_specs=[
            pl.BlockSpec((tt, dq), lambda i: (i, 0)),
            pl.BlockSpec(keys.shape, lambda i: (0, 0, 0, 0)),
        ],
        out_specs=[
            pl.BlockSpec((PEER_HEADS, 4, PEER_NKEYS, tt), lambda i: (0, 0, 0, i)),
            pl.BlockSpec((PEER_HEADS, SUBLANES, tt), lambda i: (0, 0, i)),
        ],
        out_shape=[
            jax.ShapeDtypeStruct((PEER_HEADS, 4, PEER_NKEYS, t), F32),
            jax.ShapeDtypeStruct((PEER_HEADS, SUBLANES, t), F32),
        ],
        compiler_params=_cparams("parallel"),
        name=name,
    )(q, keys)


def _peer_kernel(xn_ref, u_ref, v_ref, r_ref, thr_ref, o_ref, *, tt, et):
    j = pl.program_id(1)

    @pl.when(j == 0)
    def _():
        o_ref[...] = jnp.zeros_like(o_ref)

    s_t = lax.dot_general(u_ref[...], xn_ref[...], (((1,), (1,)), ((), ())), preferred_element_type=F32)
    na = et // PEER_NKEYS
    gates = []
    for aa in range(na):
        a = j * na + aa
        g = jnp.zeros((PEER_NKEYS, tt), F32)
        for h in range(PEER_HEADS):
            s1 = r_ref[h, 0, pl.ds(a, 1), :]
            e1 = r_ref[h, 2, pl.ds(a, 1), :]
            c = s1 + r_ref[h, 1]
            g = g + jnp.where(c >= thr_ref[h, 0:1, :], e1 * r_ref[h, 3], 0.0)
        gates.append(g)
    gate = gates[0] if na == 1 else jnp.concatenate(gates, axis=0)
    w = (_gelu(s_t) * gate).T.astype(BF16)
    o_ref[...] += jnp.dot(w, v_ref[...], preferred_element_type=F32)


def _peer(xn, u, v, r, thr, *, tt, et, name):
    t, d = xn.shape
    ne = u.shape[0]
    return pl.pallas_call(
        functools.partial(_peer_kernel, tt=tt, et=et),
        grid=(t // tt, ne // et),
        in_specs=[
            pl.BlockSpec((tt, d), lambda i, j: (i, 0)),
            pl.BlockSpec((et, d), lambda i, j: (j, 0)),
            pl.BlockSpec((et, d), lambda i, j: (j, 0)),
            pl.BlockSpec((PEER_HEADS, 4, PEER_NKEYS, tt), lambda i, j: (0, 0, 0, i)),
            pl.BlockSpec((PEER_HEADS, SUBLANES, tt), lambda i, j: (0, 0, i)),
        ],
        out_specs=pl.BlockSpec((tt, d), lambda i, j: (i, 0)),
        out_shape=jax.ShapeDtypeStruct((t, d), F32),
        compiler_params=_cparams("parallel", "arbitrary"),
        name=name,
    )(xn, u, v, r, thr)


def _ple_kernel(x1_ref, pr_ref, p_ref, g_ref, wg_ref, wp_ref, gf_ref, o_ref, x2_ref, xn_ref, *, bn):
    j = pl.program_id(1)

    @pl.when(j == 0)
    def _():
        x2 = x1_ref[...] + pr_ref[...]
        x2_ref[...] = x2
        xn_ref[...] = _rmsnorm(x2, g_ref[...]).astype(BF16)

    col = pl.multiple_of(j * bn, bn)
    gate = jax.nn.sigmoid(jnp.dot(xn_ref[...], wg_ref[...], preferred_element_type=F32))
    ple = jnp.dot(p_ref[...].astype(BF16), wp_ref[...], preferred_element_type=F32)
    o_ref[:, pl.ds(col, bn)] = x2_ref[:, pl.ds(col, bn)] + ple * gate

    @pl.when(j == pl.num_programs(1) - 1)
    def _():
        o_ref[...] = _rmsnorm(o_ref[...], gf_ref[...])


def _ple(x1, peer, p, g_ple, w_gate, w_ple, g_final, *, bm, bn, name):
    t, d = x1.shape
    dp = p.shape[1]
    return pl.pallas_call(
        functools.partial(_ple_kernel, bn=bn),
        grid=(t // bm, d // bn),
        in_specs=[
            pl.BlockSpec((bm, d), lambda i, j: (i, 0)),
            pl.BlockSpec((bm, d), lambda i, j: (i, 0)),
            pl.BlockSpec((bm, dp), lambda i, j: (i, 0)),
            pl.BlockSpec((1, d), lambda i, j: (0, 0)),
            pl.BlockSpec((d, bn), lambda i, j: (0, j)),
            pl.BlockSpec((dp, bn), lambda i, j: (0, j)),
            pl.BlockSpec((1, d), lambda i, j: (0, 0)),
        ],
        out_specs=pl.BlockSpec((bm, d), lambda i, j: (i, 0)),
        out_shape=jax.ShapeDtypeStruct((t, d), F32),
        scratch_shapes=[pltpu.VMEM((bm, d), F32), pltpu.VMEM((bm, d), BF16)],
        compiler_params=_cparams("parallel", "arbitrary"),
        name=name,
    )(x1, peer, p, g_ple, w_gate, w_ple, g_final)


def _tile(n, pref):
    return pref if n % pref == 0 else n


def _stream(x, p, pool_init, h0_re, h0_im, pos0, wts, *, tag):
    bsz, seq, d = x.shape
    t = bsz * seq
    x2d = x.reshape(t, d)
    mix_w = wts["w_in"].shape[1]
    pool_w = mix_w // 2
    ssm_w = mix_w - pool_w
    slabs, sw, two_half = wts["bbd"].shape
    half = two_half // 2

    z = _norm_matmul(x2d, wts["g_mix"], wts["w_in"], bm=_tile(t, 512), bn=_tile(mix_w, 1024),
                     emit_xn=False, name=f"in_proj_{tag}")

    y_pool = _pool_mixer(z, pool_init, wts["w_pool"], wts["pool_scale"], nseq=bsz, seq_len=seq,
                         tt=_tile(seq, 512), pos0=pos0, name=f"pool_{tag}")

    zs = z[:, pool_w:]
    if seq >= 1024:
        nb, nseq, ngroups = 1024 // SSM_BLOCK, 1, bsz
        nt = seq // 1024
        zp = zs.reshape(bsz, nt, nb, SSM_BLOCK, ssm_w).transpose(0, 1, 3, 2, 4).reshape(t, ssm_w)
    else:
        nb, nseq, ngroups = seq // SSM_BLOCK, bsz, 1
        zp = zs.reshape(bsz, nb, SSM_BLOCK, ssm_w).transpose(2, 1, 0, 3).reshape(t, ssm_w)

    def to_slabs(h):
        return h.reshape(ngroups, nseq, slabs, half).transpose(2, 0, 1, 3)

    h0 = jnp.concatenate([to_slabs(h0_re), to_slabs(h0_im)], axis=-1)
    sp, h_last = _ssm_mixer(zp, wts["bbd"], wts["cbd"], wts["pows"], wts["dsk"], h0,
                            ngroups=ngroups, nseq=nseq, nb=nb, name=f"ssm_{tag}")
    sp = _glu(sp, wts["w_glu"], wts["b_glu"], bm=_tile(t, 512), bn=_tile(ssm_w, 1024), name=f"glu_{tag}")
    if seq >= 1024:
        s = sp.reshape(bsz, nt, SSM_BLOCK, nb, ssm_w).transpose(0, 1, 3, 2, 4).reshape(t, ssm_w)
    else:
        s = sp.reshape(SSM_BLOCK, nb, bsz, ssm_w).transpose(2, 1, 0, 3).reshape(t, ssm_w)

    def from_slabs(h):
        return h.transpose(1, 2, 0, 3).reshape(bsz, slabs * half // SSM_N, SSM_N)

    new_re = from_slabs(h_last[..., :half])
    new_im = from_slabs(h_last[..., half:])

    x1 = _outproj(y_pool, s, x2d, wts["w_out"], bm=_tile(t, 512), bn=_tile(d, 1024), name=f"out_proj_{tag}")

    q, xn = _norm_matmul(x1, wts["g_ffn"], wts["w_query"], bm=_tile(t, 512), bn=_tile(wts["w_query"].shape[1], 512),
                         emit_xn=True, name=f"query_{tag}")
    tt = _tile(t, 512)
    r, thr = _route(q, wts["keys"], tt=tt, name=f"route_{tag}")
    peer = _peer(xn, wts["expert_u"], wts["expert_v"], r, thr, tt=tt, et=256, name=f"peer_{tag}")

    y = _ple(x1, peer, p.reshape(t, -1), wts["g_ple"], wts["w_ple_gate"], wts["w_ple"], wts["g_final"],
             bm=_tile(t, 256), bn=_tile(d, 1024), name=f"ple_{tag}")

    new_pool = z.reshape(bsz, seq, mix_w)[:, seq - (POOL_HALO - 1):, :pool_w]
    return y.reshape(bsz, seq, d), new_pool, new_re, new_im


def _layer_weights(i, g_mix, w_in, w_pool, pool_scale, ssm_a_re, ssm_a_im, ssm_log_dt, ssm_b_re, ssm_b_im,
                   ssm_c_re, ssm_c_im, ssm_d, w_glu, b_glu, w_out, g_ffn, w_query, peer_sub_keys,
                   expert_u, expert_v, g_ple, w_ple_gate, w_ple, g_final):
    groups, n = ssm_a_re[i].shape
    slabs = groups // SSM_SLAB_GROUPS
    pre, pim, bre, bim = _ssm_params(ssm_a_re[i], ssm_a_im[i], ssm_log_dt[i], ssm_b_re[i], ssm_b_im[i])

    def b_blocks(b):
        return b.reshape(SSM_CH, slabs, SSM_SLAB_GROUPS, n).transpose(1, 2, 0, 3)

    bbd = jnp.concatenate([_block_diag(b_blocks(bre)), _block_diag(b_blocks(bim))], axis=-1).astype(BF16)

    def c_blocks(c):
        return c.reshape(slabs, SSM_SLAB_GROUPS, SSM_CH, n).transpose(0, 1, 3, 2)

    cbd = jnp.concatenate([_block_diag(c_blocks(ssm_c_re[i])), _block_diag(c_blocks(-ssm_c_im[i]))],
                          axis=1).astype(BF16)

    def pow_rows(pw):
        return pw.reshape(pw.shape[0], slabs, SSM_SLAB_GROUPS * n).transpose(1, 0, 2)

    pows = jnp.concatenate([pow_rows(pre), pow_rows(pim)], axis=-1)
    pows = jnp.pad(pows, ((0, 0), (0, 2 * SUBLANES - pows.shape[1]), (0, 0)))
    dsk = ssm_d[i].reshape(slabs, 1, SSM_SLAB_GROUPS * SSM_CH)

    row = lambda v: v.reshape(1, -1)
    return dict(
        g_mix=row(g_mix[i]), w_in=w_in[i].astype(BF16), w_pool=w_pool[i].astype(BF16),
        pool_scale=row(pool_scale[i]), bbd=bbd, cbd=cbd, pows=pows, dsk=dsk,
        w_glu=w_glu[i].astype(BF16), b_glu=row(b_glu[i]), w_out=w_out[i].astype(BF16),
        g_ffn=row(g_ffn[i]), w_query=w_query[i].astype(BF16), keys=peer_sub_keys[i],
        expert_u=expert_u[i].astype(BF16), expert_v=expert_v[i].astype(BF16),
        g_ple=row(g_ple[i]), w_ple_gate=w_ple_gate[i].astype(BF16), w_ple=w_ple[i].astype(BF16),
        g_final=row(g_final),
    )


def kernel(x_prompt, x_sample, p_prompt, p_sample, cache_pool, state_ssm_re, state_ssm_im, g_mix, w_in, w_pool, pool_scale, ssm_a_re, ssm_a_im, ssm_log_dt, ssm_b_re, ssm_b_im, ssm_c_re, ssm_c_im, ssm_d, w_glu, b_glu, w_out, g_ffn, w_query, peer_sub_keys, expert_u, expert_v, g_ple, w_ple_gate, w_ple, g_final):
    depth = g_mix.shape[0]
    assert depth == 1, "the final rmsnorm is fused into the layer's last kernel"
    bp = x_prompt.shape[0]
    past_len = 2048
    groups, n = ssm_a_re.shape[1:]
    pool_w = cache_pool.shape[-1]

    wts = _layer_weights(0, g_mix, w_in, w_pool, pool_scale, ssm_a_re, ssm_a_im, ssm_log_dt, ssm_b_re, ssm_b_im,
                         ssm_c_re, ssm_c_im, ssm_d, w_glu, b_glu, w_out, g_ffn, w_query, peer_sub_keys,
                         expert_u, expert_v, g_ple, w_ple_gate, w_ple, g_final)

    zero_pool = jnp.zeros((bp, POOL_HALO, pool_w), F32)
    zero_h = jnp.zeros((bp, groups, n), F32)
    yp, pool_p, re_p, im_p = _stream(x_prompt, p_prompt[0], zero_pool, zero_h, zero_h, 0, wts, tag="prompt")
    pool_init = jnp.pad(cache_pool[0], ((0, 0), (1, 0), (0, 0)))
    ys, pool_s, re_s, im_s = _stream(x_sample, p_sample[0], pool_init, state_ssm_re[0], state_ssm_im[0],
                                     past_len, wts, tag="sample")
    return (yp, ys, pool_p[None], re_p[None], im_p[None], pool_s[None], re_s[None], im_s[None])
```

```python
import functools
import math

import jax
import jax.numpy as jnp
from jax import lax
from jax.experimental import pallas as pl
from jax.experimental.pallas import tpu as pltpu

F32 = jnp.float32
BF16 = jnp.bfloat16
EPS = 1e-6

SUBLANES = 8
LANES = 128
VMEM_LIMIT_BYTES = 56 * 1024 * 1024

POOL_WINDOWS = (2, 4, 8, 16)
POOL_HALO = 16
SSM_CH = 16
SSM_N = 64
SSM_BLOCK = 8
SSM_SLAB_GROUPS = 16
PEER_HEADS = 8
PEER_NKEYS = 128
PEER_DK_HALF = 128
PEER_TOPK = 16


def _cparams(*sem):
    return pltpu.CompilerParams(dimension_semantics=sem, vmem_limit_bytes=VMEM_LIMIT_BYTES)


def _rmsnorm(x, g):
    ms = jnp.mean(x * x, axis=-1, keepdims=True)
    return x * lax.rsqrt(ms + EPS) * g


def _gelu(x):
    return 0.5 * x * (1.0 + lax.erf(x * (1.0 / math.sqrt(2.0))))


def _resident(shape):
    return pl.BlockSpec(shape, lambda i: (0,) * len(shape), pipeline_mode=pl.Buffered(1))


def _rows(bm, n):
    return pl.BlockSpec((bm, n), lambda i: (i, 0))


def _norm_matmul_kernel(x_ref, g_ref, w_ref, o_ref, *rest):
    xn = _rmsnorm(x_ref[...], g_ref[...])
    if rest:
        rest[0][...] = xn.T.astype(BF16)
    o_ref[...] = jnp.dot(xn.astype(BF16), w_ref[...], preferred_element_type=F32)


def _norm_matmul(x, g, w, *, bm, emit_xn, name):
    t, d = x.shape
    n = w.shape[1]
    out_shape = [jax.ShapeDtypeStruct((t, n), F32)]
    out_specs = [_rows(bm, n)]
    if emit_xn:
        out_shape.append(jax.ShapeDtypeStruct((d, t), BF16))
        out_specs.append(pl.BlockSpec((d, bm), lambda i: (0, i)))
    res = pl.pallas_call(
        _norm_matmul_kernel,
        grid=(t // bm,),
        in_specs=[_rows(bm, d), _resident((1, d)), _resident((d, n))],
        out_specs=out_specs,
        out_shape=out_shape,
        compiler_params=_cparams("parallel"),
        name=name,
    )(x, g, w)
    return res if emit_xn else res[0]


def _pool_kernel(z_ref, init_ref, w_ref, sc_ref, o_ref, e_ref, *, tt, pos0, pg):
    i = pl.program_id(1)

    @pl.when(i == 0)
    def _():
        e_ref[0:POOL_HALO, :] = init_ref[0]

    @pl.when(i > 0)
    def _():
        e_ref[0:POOL_HALO, :] = e_ref[tt:tt + POOL_HALO, :]

    e_ref[POOL_HALO:POOL_HALO + tt, :] = z_ref[...]
    pos1 = lax.broadcasted_iota(jnp.int32, (tt, 1), 0) + (i * tt + pos0 + 1)
    for g, w in enumerate(POOL_WINDOWS):
        cols = slice(g * pg, (g + 1) * pg)
        s = e_ref[POOL_HALO:POOL_HALO + tt, cols]
        for back in range(1, w):
            s = s + e_ref[POOL_HALO - back:POOL_HALO - back + tt, cols]
        cnt = jnp.minimum(pos1, w).astype(F32)
        d = s / cnt - z_ref[:, cols]
        y = jnp.dot(d.astype(BF16), w_ref[g], preferred_element_type=F32)
        o_ref[:, cols] = (y * sc_ref[:, cols]).astype(o_ref.dtype)


def _pool_mixer(z, init, w_pool, scale, *, nseq, seq_len, tt, pos0, name):
    t = z.shape[0]
    pool_w = init.shape[-1]
    pg = pool_w // len(POOL_WINDOWS)
    nt = seq_len // tt
    return pl.pallas_call(
        functools.partial(_pool_kernel, tt=tt, pos0=pos0, pg=pg),
        grid=(nseq, nt),
        in_specs=[
            pl.BlockSpec((tt, pool_w), lambda b, i: (b * nt + i, 0)),
            pl.BlockSpec((1, POOL_HALO, pool_w), lambda b, i: (b, 0, 0)),
            pl.BlockSpec((len(POOL_WINDOWS), pg, pg), lambda b, i: (0, 0, 0)),
            pl.BlockSpec((1, pool_w), lambda b, i: (0, 0)),
        ],
        out_specs=pl.BlockSpec((tt, pool_w), lambda b, i: (b * nt + i, 0)),
        out_shape=jax.ShapeDtypeStruct((t, pool_w), BF16),
        scratch_shapes=[pltpu.VMEM((tt + POOL_HALO, pool_w), F32)],
        compiler_params=_cparams("parallel", "arbitrary"),
        name=name,
    )(z, init, w_pool, scale)


def _ssm_param_kernel(are_ref, aim_ref, ldt_ref, bre_ref, bim_ref, pre_ref, pim_ref, bbre_ref, bbim_ref):
    a_re = are_ref[...]
    a_im = aim_ref[...]
    dt = jnp.exp(ldt_ref[...])
    ar = a_re * dt
    ai = a_im * dt
    for p in range(SSM_BLOCK + 1):
        mag = jnp.exp(ar * float(p))
        pre_ref[p] = mag * jnp.cos(ai * float(p))
        pim_ref[p] = mag * jnp.sin(ai * float(p))
    x = pre_ref[1] - 1.0
    y = pim_ref[1]
    den = a_re * a_re + a_im * a_im
    cr = (x * a_re + y * a_im) / den
    ci = (y * a_re - x * a_im) / den
    for c in range(SSM_CH):
        bbre_ref[c] = cr * bre_ref[c] - ci * bim_ref[c]
        bbim_ref[c] = cr * bim_ref[c] + ci * bre_ref[c]


def _ssm_params(a_re, a_im, log_dt, b_re, b_im):
    g, n = a_re.shape
    npow = SSM_BLOCK + 1
    full = lambda *shape: pl.BlockSpec(shape, lambda: (0,) * len(shape))
    return pl.pallas_call(
        _ssm_param_kernel,
        in_specs=[full(g, n), full(g, n), full(g, 1), full(SSM_CH, g, n), full(SSM_CH, g, n)],
        out_specs=[full(npow, g, n), full(npow, g, n), full(SSM_CH, g, n), full(SSM_CH, g, n)],
        out_shape=[
            jax.ShapeDtypeStruct((npow, g, n), F32),
            jax.ShapeDtypeStruct((npow, g, n), F32),
            jax.ShapeDtypeStruct((SSM_CH, g, n), F32),
            jax.ShapeDtypeStruct((SSM_CH, g, n), F32),
        ],
        name="ssm_params",
    )(a_re, a_im, log_dt.reshape(g, 1), jnp.transpose(b_re, (2, 0, 1)), jnp.transpose(b_im, (2, 0, 1)))


def _block_diag(x):
    s, g, r, c = x.shape
    eye = jnp.eye(g, dtype=x.dtype)
    return jnp.einsum("sgrc,gh->sgrhc", x, eye).reshape(s, g * r, g * c)


def _ssm_kernel(z_ref, b_ref, c_ref, pw_ref, d_ref, h0_ref, o_ref, hl_ref, xs_ref, hin_ref, h_ref,
                *, nseq, nb, half, chunk):
    i = pl.program_id(2)
    rows = nb * nseq

    @pl.when(i == 0)
    def _():
        h_ref[...] = h0_ref[0, 0]

    u = z_ref[...]
    xs_ref[...] = jnp.dot(u.astype(BF16), b_ref[0], preferred_element_type=F32)

    def lane_chunks():
        for k in range(half // chunk):
            yield slice(k * chunk, (k + 1) * chunk), slice(half + k * chunk, half + (k + 1) * chunk)

    for l in range(1, SSM_BLOCK):
        prev = slice((l - 1) * rows, l * rows)
        cur = slice(l * rows, (l + 1) * rows)
        for re, im in lane_chunks():
            lr = pw_ref[0, 1:2, re]
            li = pw_ref[0, 1:2, im]
            pr = xs_ref[prev, re]
            pi = xs_ref[prev, im]
            xs_ref[cur, re] += pr * lr - pi * li
            xs_ref[cur, im] += pr * li + pi * lr

    l8r = pw_ref[0, SSM_BLOCK:SSM_BLOCK + 1, 0:half]
    l8i = pw_ref[0, SSM_BLOCK:SSM_BLOCK + 1, half:2 * half]
    last = (SSM_BLOCK - 1) * rows

    def step(c, carry):
        hr, hi = carry
        r0 = c * nseq
        hin_ref[pl.ds(r0, nseq), 0:half] = hr
        hin_ref[pl.ds(r0, nseq), half:2 * half] = hi
        sr = xs_ref[pl.ds(last + r0, nseq), 0:half]
        si = xs_ref[pl.ds(last + r0, nseq), half:2 * half]
        return l8r * hr - l8i * hi + sr, l8r * hi + l8i * hr + si

    hr, hi = lax.fori_loop(0, nb, step, (h_ref[:, 0:half], h_ref[:, half:2 * half]))
    h_ref[:, 0:half] = hr
    h_ref[:, half:2 * half] = hi

    for l in range(SSM_BLOCK):
        cur = slice(l * rows, (l + 1) * rows)
        for re, im in lane_chunks():
            lr = pw_ref[0, l + 1:l + 2, re]
            li = pw_ref[0, l + 1:l + 2, im]
            pr = hin_ref[:, re]
            pi = hin_ref[:, im]
            xs_ref[cur, re] += pr * lr - pi * li
            xs_ref[cur, im] += pr * li + pi * lr

    y = jnp.dot(xs_ref[...].astype(BF16), c_ref[0], preferred_element_type=F32) + d_ref[0] * u
    o_ref[...] = _gelu(y)

    @pl.when(i == pl.num_programs(2) - 1)
    def _():
        hl_ref[0, 0] = h_ref[...]


def _ssm_mixer(zp, bbd, cbd, pows, dsk, h0, *, ngroups, nseq, nb, name):
    t, ssm_w = zp.shape
    slabs = bbd.shape[0]
    sw = ssm_w // slabs
    half = bbd.shape[2] // 2
    tt = SSM_BLOCK * nb * nseq
    nt = t // (ngroups * tt)
    npow = pows.shape[1]
    kernel = functools.partial(_ssm_kernel, nseq=nseq, nb=nb, half=half, chunk=min(half, 2 * LANES))
    return pl.pallas_call(
        kernel,
        grid=(slabs, ngroups, nt),
        in_specs=[
            pl.BlockSpec((tt, sw), lambda s, b, i: (b * nt + i, s)),
            pl.BlockSpec((1, sw, 2 * half), lambda s, b, i: (s, 0, 0)),
            pl.BlockSpec((1, 2 * half, sw), lambda s, b, i: (s, 0, 0)),
            pl.BlockSpec((1, npow, 2 * half), lambda s, b, i: (s, 0, 0)),
            pl.BlockSpec((1, 1, sw), lambda s, b, i: (s, 0, 0)),
            pl.BlockSpec((1, 1, nseq, 2 * half), lambda s, b, i: (s, b, 0, 0)),
        ],
        out_specs=[
            pl.BlockSpec((tt, sw), lambda s, b, i: (b * nt + i, s)),
            pl.BlockSpec((1, 1, nseq, 2 * half), lambda s, b, i: (s, b, 0, 0)),
        ],
        out_shape=[
            jax.ShapeDtypeStruct((t, ssm_w), F32),
            jax.ShapeDtypeStruct((slabs, ngroups, nseq, 2 * half), F32),
        ],
        scratch_shapes=[
            pltpu.VMEM((tt, 2 * half), F32),
            pltpu.VMEM((nb * nseq, 2 * half), F32),
            pltpu.VMEM((nseq, 2 * half), F32),
        ],
        compiler_params=_cparams("parallel", "arbitrary", "arbitrary"),
        name=name,
    )(zp, bbd, cbd, pows, dsk, h0)


def _glu_kernel(s_ref, w_ref, b_ref, o_ref):
    s = s_ref[...]
    acc = jnp.dot(s.astype(BF16), w_ref[...], preferred_element_type=F32) + b_ref[...]
    o_ref[...] = (s * jax.nn.sigmoid(acc)).astype(o_ref.dtype)


def _glu(s, w, b, *, bm, name):
    t, k = s.shape
    n = w.shape[1]
    return pl.pallas_call(
        _glu_kernel,
        grid=(t // bm,),
        in_specs=[_rows(bm, k), _resident((k, n)), _resident((1, n))],
        out_specs=_rows(bm, n),
        out_shape=jax.ShapeDtypeStruct((t, n), BF16),
        compiler_params=_cparams("parallel"),
        name=name,
    )(s, w, b)


def _outproj_kernel(a_ref, b_ref, x_ref, w_ref, o_ref):
    lhs = jnp.concatenate([a_ref[...], b_ref[...]], axis=1)
    o_ref[...] = x_ref[...] + jnp.dot(lhs, w_ref[...], preferred_element_type=F32)


def _outproj(a, b, x, w, *, bm, name):
    t, ka = a.shape
    kb = b.shape[1]
    n = w.shape[1]
    return pl.pallas_call(
        _outproj_kernel,
        grid=(t // bm,),
        in_specs=[_rows(bm, ka), _rows(bm, kb), _rows(bm, n), _resident((ka + kb, n))],
        out_specs=_rows(bm, n),
        out_shape=jax.ShapeDtypeStruct((t, n), F32),
        compiler_params=_cparams("parallel"),
        name=name,
    )(a, b, x, w)


ROUTE_RANKS = PEER_TOPK + 1
ROUTE_PAIR_COUNTS = tuple(ROUTE_RANKS // (i + 1) for i in range(ROUTE_RANKS))


def _top_rows(x, k):
    tops = []
    for _ in range(k):
        m = jnp.max(x, axis=0, keepdims=True)
        tops.append(m)
        x = jnp.where(x == m, -jnp.inf, x)
    return tops


def _route_kernel(q_ref, k_ref, r_ref):
    for h in range(PEER_HEADS):
        scores, tops = [], []
        for p in range(2):
            c0 = (2 * h + p) * PEER_DK_HALF
            s = lax.dot_general(k_ref[h, p], q_ref[:, c0:c0 + PEER_DK_HALF], (((1,), (1,)), ((), ())),
                                precision=lax.Precision.HIGHEST, preferred_element_type=F32)
            scores.append(s)
            tops.append(_top_rows(s, ROUTE_RANKS))
        v2 = jnp.concatenate(tops[1], axis=0)
        cand = jnp.concatenate([tops[0][i] + v2[0:n] for i, n in enumerate(ROUTE_PAIR_COUNTS)], axis=0)
        best = _top_rows(cand, ROUTE_RANKS)
        zsum = jnp.zeros_like(best[0])
        for b in best[:PEER_TOPK]:
            zsum = zsum + jnp.exp(b - best[0])
        thr = 0.5 * (best[PEER_TOPK - 1] + best[PEER_TOPK])
        r_ref[h, 0] = thr - scores[0]
        r_ref[h, 1] = jnp.exp(scores[0] - tops[0][0]) / zsum
        r_ref[h, 2] = scores[1]
        r_ref[h, 3] = jnp.exp(scores[1] - tops[1][0])


def _route(q, keys, *, tt, name):
    t, dq = q.shape
    return pl.pallas_call(
        _route_kernel,
        grid=(t // tt,),
        in_specs=[
            pl.BlockSpec((tt, dq), lambda i: (i, 0)),
            pl.BlockSpec(keys.shape, lambda i: (0, 0, 0, 0)),
        ],
        out_specs=pl.BlockSpec((PEER_HEADS, 4, PEER_NKEYS, tt), lambda i: (0, 0, 0, i)),
        out_shape=jax.ShapeDtypeStruct((PEER_HEADS, 4, PEER_NKEYS, t), F32),
        compiler_params=_cparams("parallel"),
        name=name,
    )(q, keys)


PEER_SUB = 2 * PEER_NKEYS


def _peer_kernel(xnt_ref, u_ref, vt_ref, r_ref, o_ref, *, et):
    j = pl.program_id(1)

    @pl.when(j == 0)
    def _():
        o_ref[...] = jnp.zeros_like(o_ref)

    ws = []
    for k in range(et // PEER_SUB):
        s_t = jnp.dot(u_ref[k * PEER_SUB:(k + 1) * PEER_SUB, :], xnt_ref[...], preferred_element_type=F32)
        gates = []
        for aa in range(PEER_SUB // PEER_NKEYS):
            a = j * (et // PEER_NKEYS) + k * (PEER_SUB // PEER_NKEYS) + aa
            g = None
            for h in range(PEER_HEADS):
                th = r_ref[h, 0, pl.ds(a, 1), :]
                e1 = r_ref[h, 1, pl.ds(a, 1), :]
                term = jnp.where(r_ref[h, 2] >= th, r_ref[h, 3], 0.0) * e1
                g = term if g is None else g + term
            gates.append(g)
        ws.append((_gelu(s_t) * jnp.concatenate(gates, axis=0)).astype(BF16))
    o_ref[...] += jnp.dot(vt_ref[...], jnp.concatenate(ws, axis=0), preferred_element_type=F32)


def _peer(xnt, u, vt, r, *, tt, et, name):
    d, t = xnt.shape
    ne = u.shape[0]
    once = pl.Buffered(1)
    return pl.pallas_call(
        functools.partial(_peer_kernel, et=et),
        grid=(t // tt, ne // et),
        in_specs=[
            pl.BlockSpec((d, tt), lambda i, j: (0, i), pipeline_mode=once),
            pl.BlockSpec((et, d), lambda i, j: (j, 0)),
            pl.BlockSpec((d, et), lambda i, j: (0, j)),
            pl.BlockSpec((PEER_HEADS, 4, PEER_NKEYS, tt), lambda i, j: (0, 0, 0, i), pipeline_mode=once),
        ],
        out_specs=pl.BlockSpec((d, tt), lambda i, j: (0, i)),
        out_shape=jax.ShapeDtypeStruct((d, t), F32),
        compiler_params=_cparams("parallel", "arbitrary"),
        name=name,
    )(xnt, u, vt, r)


def _ple_kernel(x1_ref, pt_ref, p_ref, g_ref, wg_ref, wp_ref, gf_ref, o_ref):
    x2 = x1_ref[...] + pt_ref[...].T
    xn = _rmsnorm(x2, g_ref[...]).astype(BF16)
    gate = jax.nn.sigmoid(jnp.dot(xn, wg_ref[...], preferred_element_type=F32))
    ple = jnp.dot(p_ref[...].astype(BF16), wp_ref[...], preferred_element_type=F32)
    o_ref[...] = _rmsnorm(x2 + ple * gate, gf_ref[...])


def _ple(x1, peer_t, p, g_ple, w_gate, w_ple, g_final, *, bm, name):
    t, d = x1.shape
    dp = p.shape[1]
    return pl.pallas_call(
        _ple_kernel,
        grid=(t // bm,),
        in_specs=[
            _rows(bm, d),
            pl.BlockSpec((d, bm), lambda i: (0, i)),
            _rows(bm, dp),
            _resident((1, d)),
            _resident((d, d)),
            _resident((dp, d)),
            _resident((1, d)),
        ],
        out_specs=_rows(bm, d),
        out_shape=jax.ShapeDtypeStruct((t, d), F32),
        compiler_params=_cparams("parallel"),
        name=name,
    )(x1, peer_t, p, g_ple, w_gate, w_ple, g_final)


def _tile(n, pref):
    return pref if n % pref == 0 else n


def _stream(x, p, pool_init, h0_re, h0_im, pos0, wts, *, tag):
    bsz, seq, d = x.shape
    t = bsz * seq
    x2d = x.reshape(t, d)
    mix_w = wts["w_in"].shape[1]
    pool_w = mix_w // 2
    ssm_w = mix_w - pool_w
    slabs, sw, two_half = wts["bbd"].shape
    half = two_half // 2

    z = _norm_matmul(x2d, wts["g_mix"], wts["w_in"], bm=_tile(t, 256), emit_xn=False, name=f"in_proj_{tag}")

    y_pool = _pool_mixer(z, pool_init, wts["w_pool"], wts["pool_scale"], nseq=bsz, seq_len=seq,
                         tt=_tile(seq, 512), pos0=pos0, name=f"pool_{tag}")

    zs = z[:, pool_w:]
    if seq >= 1024:
        nb, nseq, ngroups = 1024 // SSM_BLOCK, 1, bsz
        nt = seq // 1024
        zp = zs.reshape(bsz, nt, nb, SSM_BLOCK, ssm_w).transpose(0, 1, 3, 2, 4).reshape(t, ssm_w)
    else:
        nb, nseq, ngroups = seq // SSM_BLOCK, bsz, 1
        zp = zs.reshape(bsz, nb, SSM_BLOCK, ssm_w).transpose(2, 1, 0, 3).reshape(t, ssm_w)

    def to_slabs(h):
        return h.reshape(ngroups, nseq, slabs, half).transpose(2, 0, 1, 3)

    h0 = jnp.concatenate([to_slabs(h0_re), to_slabs(h0_im)], axis=-1)
    sp, h_last = _ssm_mixer(zp, wts["bbd"], wts["cbd"], wts["pows"], wts["dsk"], h0,
                            ngroups=ngroups, nseq=nseq, nb=nb, name=f"ssm_{tag}")
    sp = _glu(sp, wts["w_glu"], wts["b_glu"], bm=_tile(t, 512), name=f"glu_{tag}")
    if seq >= 1024:
        s = sp.reshape(bsz, nt, SSM_BLOCK, nb, ssm_w).transpose(0, 1, 3, 2, 4).reshape(t, ssm_w)
    else:
        s = sp.reshape(SSM_BLOCK, nb, bsz, ssm_w).transpose(2, 1, 0, 3).reshape(t, ssm_w)

    def from_slabs(h):
        return h.transpose(1, 2, 0, 3).reshape(bsz, slabs * half // SSM_N, SSM_N)

    new_re = from_slabs(h_last[..., :half])
    new_im = from_slabs(h_last[..., half:])

    x1 = _outproj(y_pool, s, x2d, wts["w_out"], bm=_tile(t, 256), name=f"out_proj_{tag}")

    q, xnt = _norm_matmul(x1, wts["g_ffn"], wts["w_query"], bm=_tile(t, 256), emit_xn=True, name=f"query_{tag}")
    tt = _tile(t, 512)
    r = _route(q, wts["keys"], tt=tt, name=f"route_{tag}")
    peer_t = _peer(xnt, wts["expert_u"], wts["expert_vt"], r, tt=tt, et=512, name=f"peer_{tag}")

    y = _ple(x1, peer_t, p.reshape(t, -1), wts["g_ple"], wts["w_ple_gate"], wts["w_ple"], wts["g_final"],
             bm=_tile(t, 128), name=f"ple_{tag}")

    new_pool = z.reshape(bsz, seq, mix_w)[:, seq - (POOL_HALO - 1):, :pool_w]
    return y.reshape(bsz, seq, d), new_pool, new_re, new_im


def _layer_weights(i, g_mix, w_in, w_pool, pool_scale, ssm_a_re, ssm_a_im, ssm_log_dt, ssm_b_re, ssm_b_im,
                   ssm_c_re, ssm_c_im, ssm_d, w_glu, b_glu, w_out, g_ffn, w_query, peer_sub_keys,
                   expert_u, expert_v, g_ple, w_ple_gate, w_ple, g_final):
    groups, n = ssm_a_re[i].shape
    slabs = groups // SSM_SLAB_GROUPS
    pre, pim, bre, bim = _ssm_params(ssm_a_re[i], ssm_a_im[i], ssm_log_dt[i], ssm_b_re[i], ssm_b_im[i])

    def b_blocks(b):
        return b.reshape(SSM_CH, slabs, SSM_SLAB_GROUPS, n).transpose(1, 2, 0, 3)

    bbd = jnp.concatenate([_block_diag(b_blocks(bre)), _block_diag(b_blocks(bim))], axis=-1).astype(BF16)

    def c_blocks(c):
        return c.reshape(slabs, SSM_SLAB_GROUPS, SSM_CH, n).transpose(0, 1, 3, 2)

    cbd = jnp.concatenate([_block_diag(c_blocks(ssm_c_re[i])), _block_diag(c_blocks(-ssm_c_im[i]))],
                          axis=1).astype(BF16)

    def pow_rows(pw):
        return pw.reshape(pw.shape[0], slabs, SSM_SLAB_GROUPS * n).transpose(1, 0, 2)

    pows = jnp.concatenate([pow_rows(pre), pow_rows(pim)], axis=-1)
    pows = jnp.pad(pows, ((0, 0), (0, 2 * SUBLANES - pows.shape[1]), (0, 0)))
    dsk = ssm_d[i].reshape(slabs, 1, SSM_SLAB_GROUPS * SSM_CH)

    row = lambda v: v.reshape(1, -1)
    return dict(
        g_mix=row(g_mix[i]), w_in=w_in[i].astype(BF16), w_pool=w_pool[i].astype(BF16),
        pool_scale=row(pool_scale[i]), bbd=bbd, cbd=cbd, pows=pows, dsk=dsk,
        w_glu=w_glu[i].astype(BF16), b_glu=row(b_glu[i]), w_out=w_out[i].astype(BF16),
        g_ffn=row(g_ffn[i]), w_query=w_query[i].astype(BF16), keys=peer_sub_keys[i],
        expert_u=expert_u[i].astype(BF16), expert_vt=expert_v[i].T.astype(BF16),
        g_ple=row(g_ple[i]), w_ple_gate=w_ple_gate[i].astype(BF16), w_ple=w_ple[i].astype(BF16),
        g_final=row(g_final),
    )


def kernel(x_prompt, x_sample, p_prompt, p_sample, cache_pool, state_ssm_re, state_ssm_im, g_mix, w_in, w_pool, pool_scale, ssm_a_re, ssm_a_im, ssm_log_dt, ssm_b_re, ssm_b_im, ssm_c_re, ssm_c_im, ssm_d, w_glu, b_glu, w_out, g_ffn, w_query, peer_sub_keys, expert_u, expert_v, g_ple, w_ple_gate, w_ple, g_final):
    depth = g_mix.shape[0]
    assert depth == 1, "the final rmsnorm is fused into the layer's last kernel"
    bp = x_prompt.shape[0]
    past_len = 2048
    groups, n = ssm_a_re.shape[1:]
    pool_w = cache_pool.shape[-1]

    wts = _layer_weights(0, g_mix, w_in, w_pool, pool_scale, ssm_a_re, ssm_a_im, ssm_log_dt, ssm_b_re, ssm_b_im,
                         ssm_c_re, ssm_c_im, ssm_d, w_glu, b_glu, w_out, g_ffn, w_query, peer_sub_keys,
                         expert_u, expert_v, g_ple, w_ple_gate, w_ple, g_final)

    zero_pool = jnp.zeros((bp, POOL_HALO, pool_w), F32)
    zero_h = jnp.zeros((bp, groups, n), F32)
    yp, pool_p, re_p, im_p = _stream(x_prompt, p_prompt[0], zero_pool, zero_h, zero_h, 0, wts, tag="prompt")
    pool_init = jnp.pad(cache_pool[0], ((0, 0), (1, 0), (0, 0)))
    ys, pool_s, re_s, im_s = _stream(x_sample, p_sample[0], pool_init, state_ssm_re[0], state_ssm_im[0],
                                     past_len, wts, tag="sample")
    return (yp, ys, pool_p[None], re_p[None], im_p[None], pool_s[None], re_s[None], im_s[None])
```

```python
import functools
import math

import jax
import jax.numpy as jnp
from jax import lax
from jax.experimental import pallas as pl
from jax.experimental.pallas import tpu as pltpu

F32 = jnp.float32
BF16 = jnp.bfloat16
EPS = 1e-6

SUBLANES = 8
LANES = 128
VMEM_LIMIT_BYTES = 56 * 1024 * 1024

POOL_WINDOWS = (2, 4, 8, 16)
POOL_HALO = 16
SSM_CH = 16
SSM_N = 64
SSM_BLOCK = 8
SSM_SLAB_GROUPS = 16
PEER_HEADS = 8
PEER_NKEYS = 128
PEER_DK_HALF = 128
PEER_TOPK = 16
PEER_EXPERT_TILE = 512


def _cparams(*sem):
    return pltpu.CompilerParams(dimension_semantics=sem, vmem_limit_bytes=VMEM_LIMIT_BYTES)


def _rmsnorm(x, g):
    ms = jnp.mean(x * x, axis=-1, keepdims=True)
    return x * lax.rsqrt(ms + EPS) * g


def _gelu(x):
    return 0.5 * x * (1.0 + lax.erf(x * (1.0 / math.sqrt(2.0))))


def _resident(shape):
    return pl.BlockSpec(shape, lambda i: (0,) * len(shape), pipeline_mode=pl.Buffered(1))


def _rows(bm, n):
    return pl.BlockSpec((bm, n), lambda i: (i, 0))


def _norm_matmul_kernel(x_ref, g_ref, w_ref, o_ref, *rest):
    xn = _rmsnorm(x_ref[...], g_ref[...])
    if rest:
        rest[0][...] = xn.T.astype(BF16)
    o_ref[...] = jnp.dot(xn.astype(BF16), w_ref[...], preferred_element_type=F32)


def _norm_matmul(x, g, w, *, bm, emit_xn, name):
    t, d = x.shape
    n = w.shape[1]
    out_shape = [jax.ShapeDtypeStruct((t, n), F32)]
    out_specs = [_rows(bm, n)]
    if emit_xn:
        out_shape.append(jax.ShapeDtypeStruct((d, t), BF16))
        out_specs.append(pl.BlockSpec((d, bm), lambda i: (0, i)))
    res = pl.pallas_call(
        _norm_matmul_kernel,
        grid=(t // bm,),
        in_specs=[_rows(bm, d), _resident((1, d)), _resident((d, n))],
        out_specs=out_specs,
        out_shape=out_shape,
        compiler_params=_cparams("parallel"),
        name=name,
    )(x, g, w)
    return res if emit_xn else res[0]


def _pool_kernel(z_ref, init_ref, w_ref, sc_ref, o_ref, e_ref, *, tt, pos0, pg):
    i = pl.program_id(1)

    @pl.when(i == 0)
    def _():
        e_ref[0:POOL_HALO, :] = init_ref[0]

    @pl.when(i > 0)
    def _():
        e_ref[0:POOL_HALO, :] = e_ref[tt:tt + POOL_HALO, :]

    e_ref[POOL_HALO:POOL_HALO + tt, :] = z_ref[...]
    pos1 = lax.broadcasted_iota(jnp.int32, (tt, 1), 0) + (i * tt + pos0 + 1)
    for g, w in enumerate(POOL_WINDOWS):
        cols = slice(g * pg, (g + 1) * pg)
        s = e_ref[POOL_HALO:POOL_HALO + tt, cols]
        for back in range(1, w):
            s = s + e_ref[POOL_HALO - back:POOL_HALO - back + tt, cols]
        cnt = jnp.minimum(pos1, w).astype(F32)
        d = s / cnt - z_ref[:, cols]
        y = jnp.dot(d.astype(BF16), w_ref[g], preferred_element_type=F32)
        o_ref[:, cols] = (y * sc_ref[:, cols]).astype(o_ref.dtype)


def _pool_mixer(z, init, w_pool, scale, *, nseq, seq_len, tt, pos0, name):
    t = z.shape[0]
    pool_w = init.shape[-1]
    pg = pool_w // len(POOL_WINDOWS)
    nt = seq_len // tt
    return pl.pallas_call(
        functools.partial(_pool_kernel, tt=tt, pos0=pos0, pg=pg),
        grid=(nseq, nt),
        in_specs=[
            pl.BlockSpec((tt, pool_w), lambda b, i: (b * nt + i, 0)),
            pl.BlockSpec((1, POOL_HALO, pool_w), lambda b, i: (b, 0, 0)),
            pl.BlockSpec((len(POOL_WINDOWS), pg, pg), lambda b, i: (0, 0, 0)),
            pl.BlockSpec((1, pool_w), lambda b, i: (0, 0)),
        ],
        out_specs=pl.BlockSpec((tt, pool_w), lambda b, i: (b * nt + i, 0)),
        out_shape=jax.ShapeDtypeStruct((t, pool_w), BF16),
        scratch_shapes=[pltpu.VMEM((tt + POOL_HALO, pool_w), F32)],
        compiler_params=_cparams("parallel", "arbitrary"),
        name=name,
    )(z, init, w_pool, scale)


def _ssm_param_kernel(are_ref, aim_ref, ldt_ref, bre_ref, bim_ref, pre_ref, pim_ref, bbre_ref, bbim_ref):
    a_re = are_ref[...]
    a_im = aim_ref[...]
    dt = jnp.exp(ldt_ref[...])
    ar = a_re * dt
    ai = a_im * dt
    for p in range(SSM_BLOCK + 1):
        mag = jnp.exp(ar * float(p))
        pre_ref[p] = mag * jnp.cos(ai * float(p))
        pim_ref[p] = mag * jnp.sin(ai * float(p))
    x = pre_ref[1] - 1.0
    y = pim_ref[1]
    den = a_re * a_re + a_im * a_im
    cr = (x * a_re + y * a_im) / den
    ci = (y * a_re - x * a_im) / den
    for c in range(SSM_CH):
        bbre_ref[c] = cr * bre_ref[c] - ci * bim_ref[c]
        bbim_ref[c] = cr * bim_ref[c] + ci * bre_ref[c]


def _ssm_params(a_re, a_im, log_dt, b_re, b_im):
    g, n = a_re.shape
    npow = SSM_BLOCK + 1
    full = lambda *shape: pl.BlockSpec(shape, lambda: (0,) * len(shape))
    return pl.pallas_call(
        _ssm_param_kernel,
        in_specs=[full(g, n), full(g, n), full(g, 1), full(SSM_CH, g, n), full(SSM_CH, g, n)],
        out_specs=[full(npow, g, n), full(npow, g, n), full(SSM_CH, g, n), full(SSM_CH, g, n)],
        out_shape=[
            jax.ShapeDtypeStruct((npow, g, n), F32),
            jax.ShapeDtypeStruct((npow, g, n), F32),
            jax.ShapeDtypeStruct((SSM_CH, g, n), F32),
            jax.ShapeDtypeStruct((SSM_CH, g, n), F32),
        ],
        name="ssm_params",
    )(a_re, a_im, log_dt.reshape(g, 1), jnp.transpose(b_re, (2, 0, 1)), jnp.transpose(b_im, (2, 0, 1)))


def _block_diag(x):
    s, g, r, c = x.shape
    eye = jnp.eye(g, dtype=x.dtype)
    return jnp.einsum("sgrc,gh->sgrhc", x, eye).reshape(s, g * r, g * c)


def _ssm_kernel(z_ref, b_ref, c_ref, pw_ref, d_ref, h0_ref, o_ref, hl_ref, xs_ref, hin_ref, h_ref,
                *, nseq, nb, half, chunk):
    i = pl.program_id(2)
    rows = nb * nseq

    @pl.when(i == 0)
    def _():
        h_ref[...] = h0_ref[0, 0]

    u = z_ref[...]
    ub = u.astype(BF16)

    def lane_chunks():
        for k in range(half // chunk):
            yield slice(k * chunk, (k + 1) * chunk), slice(half + k * chunk, half + (k + 1) * chunk)

    for re, im in lane_chunks():
        xs_ref[:, re] = jnp.dot(ub, b_ref[0, :, re], preferred_element_type=F32)
        xs_ref[:, im] = jnp.dot(ub, b_ref[0, :, im], preferred_element_type=F32)
        lr = pw_ref[0, 1:2, re]
        li = pw_ref[0, 1:2, im]
        for l in range(1, SSM_BLOCK):
            prev = slice((l - 1) * rows, l * rows)
            cur = slice(l * rows, (l + 1) * rows)
            pr = xs_ref[prev, re]
            pi = xs_ref[prev, im]
            xs_ref[cur, re] += pr * lr - pi * li
            xs_ref[cur, im] += pr * li + pi * lr

    l8r = pw_ref[0, SSM_BLOCK:SSM_BLOCK + 1, 0:half]
    l8i = pw_ref[0, SSM_BLOCK:SSM_BLOCK + 1, half:2 * half]
    last = (SSM_BLOCK - 1) * rows

    def step(c, carry):
        hr, hi = carry
        r0 = c * nseq
        hin_ref[pl.ds(r0, nseq), 0:half] = hr
        hin_ref[pl.ds(r0, nseq), half:2 * half] = hi
        sr = xs_ref[pl.ds(last + r0, nseq), 0:half]
        si = xs_ref[pl.ds(last + r0, nseq), half:2 * half]
        return l8r * hr - l8i * hi + sr, l8r * hi + l8i * hr + si

    hr, hi = lax.fori_loop(0, nb, step, (h_ref[:, 0:half], h_ref[:, half:2 * half]), unroll=4)
    h_ref[:, 0:half] = hr
    h_ref[:, half:2 * half] = hi

    y = d_ref[0] * u
    for re, im in lane_chunks():
        pr = hin_ref[:, re]
        pi = hin_ref[:, im]
        for l in range(SSM_BLOCK):
            cur = slice(l * rows, (l + 1) * rows)
            lr = pw_ref[0, l + 1:l + 2, re]
            li = pw_ref[0, l + 1:l + 2, im]
            xs_ref[cur, re] += pr * lr - pi * li
            xs_ref[cur, im] += pr * li + pi * lr
        y = y + jnp.dot(xs_ref[:, re].astype(BF16), c_ref[0, re, :], preferred_element_type=F32)
        y = y + jnp.dot(xs_ref[:, im].astype(BF16), c_ref[0, im, :], preferred_element_type=F32)
    o_ref[...] = _gelu(y)

    @pl.when(i == pl.num_programs(2) - 1)
    def _():
        hl_ref[0, 0] = h_ref[...]


def _ssm_mixer(zp, bbd, cbd, pows, dsk, h0, *, ngroups, nseq, nb, name):
    t, ssm_w = zp.shape
    slabs = bbd.shape[0]
    sw = ssm_w // slabs
    half = bbd.shape[2] // 2
    tt = SSM_BLOCK * nb * nseq
    nt = t // (ngroups * tt)
    npow = pows.shape[1]
    kernel = functools.partial(_ssm_kernel, nseq=nseq, nb=nb, half=half, chunk=min(half, 2 * LANES))
    return pl.pallas_call(
        kernel,
        grid=(slabs, ngroups, nt),
        in_specs=[
            pl.BlockSpec((tt, sw), lambda s, b, i: (b * nt + i, s)),
            pl.BlockSpec((1, sw, 2 * half), lambda s, b, i: (s, 0, 0)),
            pl.BlockSpec((1, 2 * half, sw), lambda s, b, i: (s, 0, 0)),
            pl.BlockSpec((1, npow, 2 * half), lambda s, b, i: (s, 0, 0)),
            pl.BlockSpec((1, 1, sw), lambda s, b, i: (s, 0, 0)),
            pl.BlockSpec((1, 1, nseq, 2 * half), lambda s, b, i: (s, b, 0, 0)),
        ],
        out_specs=[
            pl.BlockSpec((tt, sw), lambda s, b, i: (b * nt + i, s)),
            pl.BlockSpec((1, 1, nseq, 2 * half), lambda s, b, i: (s, b, 0, 0)),
        ],
        out_shape=[
            jax.ShapeDtypeStruct((t, ssm_w), F32),
            jax.ShapeDtypeStruct((slabs, ngroups, nseq, 2 * half), F32),
        ],
        scratch_shapes=[
            pltpu.VMEM((tt, 2 * half), F32),
            pltpu.VMEM((nb * nseq, 2 * half), F32),
            pltpu.VMEM((nseq, 2 * half), F32),
        ],
        compiler_params=_cparams("parallel", "arbitrary", "arbitrary"),
        name=name,
    )(zp, bbd, cbd, pows, dsk, h0)


def _glu_kernel(s_ref, w_ref, b_ref, o_ref):
    s = s_ref[...]
    acc = jnp.dot(s.astype(BF16), w_ref[...], preferred_element_type=F32) + b_ref[...]
    o_ref[...] = (s * jax.nn.sigmoid(acc)).astype(o_ref.dtype)


def _glu(s, w, b, *, bm, name):
    t, k = s.shape
    n = w.shape[1]
    return pl.pallas_call(
        _glu_kernel,
        grid=(t // bm,),
        in_specs=[_rows(bm, k), _resident((k, n)), _resident((1, n))],
        out_specs=_rows(bm, n),
        out_shape=jax.ShapeDtypeStruct((t, n), BF16),
        compiler_params=_cparams("parallel"),
        name=name,
    )(s, w, b)


def _outproj_kernel(a_ref, b_ref, x_ref, w_ref, o_ref):
    lhs = jnp.concatenate([a_ref[...], b_ref[...]], axis=1)
    o_ref[...] = x_ref[...] + jnp.dot(lhs, w_ref[...], preferred_element_type=F32)


def _outproj(a, b, x, w, *, bm, name):
    t, ka = a.shape
    kb = b.shape[1]
    n = w.shape[1]
    return pl.pallas_call(
        _outproj_kernel,
        grid=(t // bm,),
        in_specs=[_rows(bm, ka), _rows(bm, kb), _rows(bm, n), _resident((ka + kb, n))],
        out_specs=_rows(bm, n),
        out_shape=jax.ShapeDtypeStruct((t, n), F32),
        compiler_params=_cparams("parallel"),
        name=name,
    )(a, b, x, w)


ROUTE_RANKS = PEER_TOPK + 1
ROUTE_PAIR_COUNTS = tuple(ROUTE_RANKS // (i + 1) for i in range(ROUTE_RANKS))


def _top_rows(x, k):
    tops = []
    for _ in range(k):
        m = jnp.max(x, axis=0, keepdims=True)
        tops.append(m)
        x = jnp.where(x == m, -jnp.inf, x)
    return tops


def _route_kernel(q_ref, k_ref, r_ref):
    for h in range(PEER_HEADS):
        scores, tops = [], []
        for p in range(2):
            c0 = (2 * h + p) * PEER_DK_HALF
            s = lax.dot_general(k_ref[h, p], q_ref[:, c0:c0 + PEER_DK_HALF], (((1,), (1,)), ((), ())),
                                precision=lax.Precision.HIGHEST, preferred_element_type=F32)
            scores.append(s)
            tops.append(_top_rows(s, ROUTE_RANKS))
        v2 = jnp.concatenate(tops[1], axis=0)
        cand = jnp.concatenate([tops[0][i] + v2[0:n] for i, n in enumerate(ROUTE_PAIR_COUNTS)], axis=0)
        best = _top_rows(cand, ROUTE_RANKS)
        zsum = jnp.zeros_like(best[0])
        for b in best[:PEER_TOPK]:
            zsum = zsum + jnp.exp(b - best[0])
        thr = 0.5 * (best[PEER_TOPK - 1] + best[PEER_TOPK])
        r_ref[h, 0] = thr - scores[0]
        r_ref[h, 1] = jnp.exp(scores[0] - tops[0][0]) / zsum
        r_ref[h, 2] = scores[1]
        r_ref[h, 3] = jnp.exp(scores[1] - tops[1][0])


def _route(q, keys, *, tt, name):
    t, dq = q.shape
    return pl.pallas_call(
        _route_kernel,
        grid=(t // tt,),
        in_specs=[
            pl.BlockSpec((tt, dq), lambda i: (i, 0)),
            pl.BlockSpec(keys.shape, lambda i: (0, 0, 0, 0)),
        ],
        out_specs=pl.BlockSpec((PEER_HEADS, 4, PEER_NKEYS, tt), lambda i: (0, 0, 0, i)),
        out_shape=jax.ShapeDtypeStruct((PEER_HEADS, 4, PEER_NKEYS, t), F32),
        compiler_params=_cparams("parallel"),
        name=name,
    )(q, keys)


PEER_SUB = 2 * PEER_NKEYS


def _peer_kernel(xnt_ref, u_ref, vt_ref, r_ref, o_ref, *, et):
    j = pl.program_id(1)

    @pl.when(j == 0)
    def _():
        o_ref[...] = jnp.zeros_like(o_ref)

    ws = []
    for k in range(et // PEER_SUB):
        s_t = jnp.dot(u_ref[k * PEER_SUB:(k + 1) * PEER_SUB, :], xnt_ref[...], preferred_element_type=F32)
        gates = []
        for aa in range(PEER_SUB // PEER_NKEYS):
            a = j * (et // PEER_NKEYS) + k * (PEER_SUB // PEER_NKEYS) + aa
            g = None
            for h in range(PEER_HEADS):
                th = r_ref[h, 0, pl.ds(a, 1), :]
                e1 = r_ref[h, 1, pl.ds(a, 1), :]
                term = jnp.where(r_ref[h, 2] >= th, r_ref[h, 3], 0.0) * e1
                g = term if g is None else g + term
            gates.append(g)
        ws.append((_gelu(s_t) * jnp.concatenate(gates, axis=0)).astype(BF16))
    o_ref[...] += jnp.dot(vt_ref[0], jnp.concatenate(ws, axis=0), preferred_element_type=F32)


def _peer(xnt, u, vt, r, *, tt, name):
    d, t = xnt.shape
    nj, _, et = vt.shape
    once = pl.Buffered(1)
    return pl.pallas_call(
        functools.partial(_peer_kernel, et=et),
        grid=(t // tt, nj),
        in_specs=[
            pl.BlockSpec((d, tt), lambda i, j: (0, i), pipeline_mode=once),
            pl.BlockSpec((et, d), lambda i, j: (j, 0)),
            pl.BlockSpec((1, d, et), lambda i, j: (j, 0, 0)),
            pl.BlockSpec((PEER_HEADS, 4, PEER_NKEYS, tt), lambda i, j: (0, 0, 0, i), pipeline_mode=once),
        ],
        out_specs=pl.BlockSpec((d, tt), lambda i, j: (0, i)),
        out_shape=jax.ShapeDtypeStruct((d, t), F32),
        compiler_params=_cparams("parallel", "arbitrary"),
        name=name,
    )(xnt, u, vt, r)


def _ple_kernel(x1_ref, pt_ref, p_ref, g_ref, wg_ref, wp_ref, gf_ref, o_ref):
    x2 = x1_ref[...] + pt_ref[...].T
    xn = _rmsnorm(x2, g_ref[...]).astype(BF16)
    gate = jax.nn.sigmoid(jnp.dot(xn, wg_ref[...], preferred_element_type=F32))
    ple = jnp.dot(p_ref[...].astype(BF16), wp_ref[...], preferred_element_type=F32)
    o_ref[...] = _rmsnorm(x2 + ple * gate, gf_ref[...])


def _ple(x1, peer_t, p, g_ple, w_gate, w_ple, g_final, *, bm, name):
    t, d = x1.shape
    dp = p.shape[1]
    return pl.pallas_call(
        _ple_kernel,
        grid=(t // bm,),
        in_specs=[
            _rows(bm, d),
            pl.BlockSpec((d, bm), lambda i: (0, i)),
            _rows(bm, dp),
            _resident((1, d)),
            _resident((d, d)),
            _resident((dp, d)),
            _resident((1, d)),
        ],
        out_specs=_rows(bm, d),
        out_shape=jax.ShapeDtypeStruct((t, d), F32),
        compiler_params=_cparams("parallel"),
        name=name,
    )(x1, peer_t, p, g_ple, w_gate, w_ple, g_final)


def _tile(n, pref):
    return pref if n % pref == 0 else n


def _stream(x, p, pool_init, h0_re, h0_im, pos0, wts, *, tag):
    bsz, seq, d = x.shape
    t = bsz * seq
    x2d = x.reshape(t, d)
    mix_w = wts["w_in"].shape[1]
    pool_w = mix_w // 2
    ssm_w = mix_w - pool_w
    slabs, sw, two_half = wts["bbd"].shape
    half = two_half // 2

    z = _norm_matmul(x2d, wts["g_mix"], wts["w_in"], bm=_tile(t, 256), emit_xn=False, name=f"in_proj_{tag}")

    y_pool = _pool_mixer(z, pool_init, wts["w_pool"], wts["pool_scale"], nseq=bsz, seq_len=seq,
                         tt=_tile(seq, 512), pos0=pos0, name=f"pool_{tag}")

    zs = z[:, pool_w:]
    if seq >= 1024:
        nb, nseq, ngroups = 1024 // SSM_BLOCK, 1, bsz
        nt = seq // 1024
        zp = zs.reshape(bsz, nt, nb, SSM_BLOCK, ssm_w).transpose(0, 1, 3, 2, 4).reshape(t, ssm_w)
    else:
        nb, nseq, ngroups = seq // SSM_BLOCK, bsz, 1
        zp = zs.reshape(bsz, nb, SSM_BLOCK, ssm_w).transpose(2, 1, 0, 3).reshape(t, ssm_w)

    def to_slabs(h):
        return h.reshape(ngroups, nseq, slabs, half).transpose(2, 0, 1, 3)

    h0 = jnp.concatenate([to_slabs(h0_re), to_slabs(h0_im)], axis=-1)
    sp, h_last = _ssm_mixer(zp, wts["bbd"], wts["cbd"], wts["pows"], wts["dsk"], h0,
                            ngroups=ngroups, nseq=nseq, nb=nb, name=f"ssm_{tag}")
    sp = _glu(sp, wts["w_glu"], wts["b_glu"], bm=_tile(t, 512), name=f"glu_{tag}")
    if seq >= 1024:
        s = sp.reshape(bsz, nt, SSM_BLOCK, nb, ssm_w).transpose(0, 1, 3, 2, 4).reshape(t, ssm_w)
    else:
        s = sp.reshape(SSM_BLOCK, nb, bsz, ssm_w).transpose(2, 1, 0, 3).reshape(t, ssm_w)

    def from_slabs(h):
        return h.transpose(1, 2, 0, 3).reshape(bsz, slabs * half // SSM_N, SSM_N)

    new_re = from_slabs(h_last[..., :half])
    new_im = from_slabs(h_last[..., half:])

    x1 = _outproj(y_pool, s, x2d, wts["w_out"], bm=_tile(t, 256), name=f"out_proj_{tag}")

    q, xnt = _norm_matmul(x1, wts["g_ffn"], wts["w_query"], bm=_tile(t, 256), emit_xn=True, name=f"query_{tag}")
    tt = _tile(t, 512)
    r = _route(q, wts["keys"], tt=tt, name=f"route_{tag}")
    peer_t = _peer(xnt, wts["expert_u"], wts["expert_vt"], r, tt=tt, name=f"peer_{tag}")

    y = _ple(x1, peer_t, p.reshape(t, -1), wts["g_ple"], wts["w_ple_gate"], wts["w_ple"], wts["g_final"],
             bm=_tile(t, 128), name=f"ple_{tag}")

    new_pool = z.reshape(bsz, seq, mix_w)[:, seq - (POOL_HALO - 1):, :pool_w]
    return y.reshape(bsz, seq, d), new_pool, new_re, new_im


def _layer_weights(i, g_mix, w_in, w_pool, pool_scale, ssm_a_re, ssm_a_im, ssm_log_dt, ssm_b_re, ssm_b_im,
                   ssm_c_re, ssm_c_im, ssm_d, w_glu, b_glu, w_out, g_ffn, w_query, peer_sub_keys,
                   expert_u, expert_v, g_ple, w_ple_gate, w_ple, g_final):
    groups, n = ssm_a_re[i].shape
    slabs = groups // SSM_SLAB_GROUPS
    pre, pim, bre, bim = _ssm_params(ssm_a_re[i], ssm_a_im[i], ssm_log_dt[i], ssm_b_re[i], ssm_b_im[i])

    def b_blocks(b):
        return b.reshape(SSM_CH, slabs, SSM_SLAB_GROUPS, n).transpose(1, 2, 0, 3)

    bbd = jnp.concatenate([_block_diag(b_blocks(bre)), _block_diag(b_blocks(bim))], axis=-1).astype(BF16)

    def c_blocks(c):
        return c.reshape(slabs, SSM_SLAB_GROUPS, SSM_CH, n).transpose(0, 1, 3, 2)

    cbd = jnp.concatenate([_block_diag(c_blocks(ssm_c_re[i])), _block_diag(c_blocks(-ssm_c_im[i]))],
                          axis=1).astype(BF16)

    def pow_rows(pw):
        return pw.reshape(pw.shape[0], slabs, SSM_SLAB_GROUPS * n).transpose(1, 0, 2)

    pows = jnp.concatenate([pow_rows(pre), pow_rows(pim)], axis=-1)
    pows = jnp.pad(pows, ((0, 0), (0, 2 * SUBLANES - pows.shape[1]), (0, 0)))
    dsk = ssm_d[i].reshape(slabs, 1, SSM_SLAB_GROUPS * SSM_CH)

    ne, d = expert_v[i].shape
    expert_vt = expert_v[i].reshape(ne // PEER_EXPERT_TILE, PEER_EXPERT_TILE, d).transpose(0, 2, 1).astype(BF16)

    row = lambda v: v.reshape(1, -1)
    return dict(
        g_mix=row(g_mix[i]), w_in=w_in[i].astype(BF16), w_pool=w_pool[i].astype(BF16),
        pool_scale=row(pool_scale[i]), bbd=bbd, cbd=cbd, pows=pows, dsk=dsk,
        w_glu=w_glu[i].astype(BF16), b_glu=row(b_glu[i]), w_out=w_out[i].astype(BF16),
        g_ffn=row(g_ffn[i]), w_query=w_query[i].astype(BF16), keys=peer_sub_keys[i],
        expert_u=expert_u[i].astype(BF16), expert_vt=expert_vt,
        g_ple=row(g_ple[i]), w_ple_gate=w_ple_gate[i].astype(BF16), w_ple=w_ple[i].astype(BF16),
        g_final=row(g_final),
    )


def kernel(x_prompt, x_sample, p_prompt, p_sample, cache_pool, state_ssm_re, state_ssm_im, g_mix, w_in, w_pool, pool_scale, ssm_a_re, ssm_a_im, ssm_log_dt, ssm_b_re, ssm_b_im, ssm_c_re, ssm_c_im, ssm_d, w_glu, b_glu, w_out, g_ffn, w_query, peer_sub_keys, expert_u, expert_v, g_ple, w_ple_gate, w_ple, g_final):
    depth = g_mix.shape[0]
    assert depth == 1, "the final rmsnorm is fused into the layer's last kernel"
    bp = x_prompt.shape[0]
    past_len = 2048
    groups, n = ssm_a_re.shape[1:]
    pool_w = cache_pool.shape[-1]

    wts = _layer_weights(0, g_mix, w_in, w_pool, pool_scale, ssm_a_re, ssm_a_im, ssm_log_dt, ssm_b_re, ssm_b_im,
                         ssm_c_re, ssm_c_im, ssm_d, w_glu, b_glu, w_out, g_ffn, w_query, peer_sub_keys,
                         expert_u, expert_v, g_ple, w_ple_gate, w_ple, g_final)

    zero_pool = jnp.zeros((bp, POOL_HALO, pool_w), F32)
    zero_h = jnp.zeros((bp, groups, n), F32)
    yp, pool_p, re_p, im_p = _stream(x_prompt, p_prompt[0], zero_pool, zero_h, zero_h, 0, wts, tag="prompt")
    pool_init = jnp.pad(cache_pool[0], ((0, 0), (1, 0), (0, 0)))
    ys, pool_s, re_s, im_s = _stream(x_sample, p_sample[0], pool_init, state_ssm_re[0], state_ssm_im[0],
                                     past_len, wts, tag="sample")
    return (yp, ys, pool_p[None], re_p[None], im_p[None], pool_s[None], re_s[None], im_s[None])
```

```python
import functools
import math

import jax
import jax.numpy as jnp
from jax import lax
from jax.experimental import pallas as pl
from jax.experimental.pallas import tpu as pltpu

F32 = jnp.float32
BF16 = jnp.bfloat16
EPS = 1e-6

SUBLANES = 8
LANES = 128
VMEM_LIMIT_BYTES = 60 * 1024 * 1024

POOL_WINDOWS = (2, 4, 8, 16)
POOL_HALO = 16
SSM_CH = 16
SSM_N = 64
SSM_BLOCK = 8
SSM_SLAB_GROUPS = 16
PEER_HEADS = 8
PEER_NKEYS = 128
PEER_DK_HALF = 128
PEER_TOPK = 16
PEER_EXPERT_TILE = 1024


def _cparams(*sem):
    return pltpu.CompilerParams(dimension_semantics=sem, vmem_limit_bytes=VMEM_LIMIT_BYTES)


def _rmsnorm(x, g):
    ms = jnp.mean(x * x, axis=-1, keepdims=True)
    return x * lax.rsqrt(ms + EPS) * g


def _gelu(x):
    return 0.5 * x * (1.0 + lax.erf(x * (1.0 / math.sqrt(2.0))))


def _resident(shape):
    return pl.BlockSpec(shape, lambda i: (0,) * len(shape), pipeline_mode=pl.Buffered(1))


def _rows(bm, n):
    return pl.BlockSpec((bm, n), lambda i: (i, 0))


def _norm_matmul_kernel(x_ref, g_ref, w_ref, o_ref, *rest):
    xn = _rmsnorm(x_ref[...], g_ref[...])
    if rest:
        rest[0][...] = xn.T.astype(BF16)
    o_ref[...] = jnp.dot(xn.astype(BF16), w_ref[...], preferred_element_type=F32)


def _norm_matmul(x, g, w, *, bm, emit_xn, name):
    t, d = x.shape
    n = w.shape[1]
    out_shape = [jax.ShapeDtypeStruct((t, n), F32)]
    out_specs = [_rows(bm, n)]
    if emit_xn:
        out_shape.append(jax.ShapeDtypeStruct((d, t), BF16))
        out_specs.append(pl.BlockSpec((d, bm), lambda i: (0, i)))
    res = pl.pallas_call(
        _norm_matmul_kernel,
        grid=(t // bm,),
        in_specs=[_rows(bm, d), _resident((1, d)), _resident((d, n))],
        out_specs=out_specs,
        out_shape=out_shape,
        compiler_params=_cparams("parallel"),
        name=name,
    )(x, g, w)
    return res if emit_xn else res[0]


def _pool_kernel(z_ref, init_ref, w_ref, sc_ref, o_ref, e_ref, *, tt, pos0, pg):
    i = pl.program_id(1)

    @pl.when(i == 0)
    def _():
        e_ref[0:POOL_HALO, :] = init_ref[0]

    @pl.when(i > 0)
    def _():
        e_ref[0:POOL_HALO, :] = e_ref[tt:tt + POOL_HALO, :]

    e_ref[POOL_HALO:POOL_HALO + tt, :] = z_ref[...]
    pos1 = lax.broadcasted_iota(jnp.int32, (tt, 1), 0) + (i * tt + pos0 + 1)
    for g, w in enumerate(POOL_WINDOWS):
        cols = slice(g * pg, (g + 1) * pg)
        s = e_ref[POOL_HALO:POOL_HALO + tt, cols]
        for back in range(1, w):
            s = s + e_ref[POOL_HALO - back:POOL_HALO - back + tt, cols]
        cnt = jnp.minimum(pos1, w).astype(F32)
        d = s / cnt - z_ref[:, cols]
        y = jnp.dot(d.astype(BF16), w_ref[g], preferred_element_type=F32)
        o_ref[:, cols] = (y * sc_ref[:, cols]).astype(o_ref.dtype)


def _pool_mixer(z, init, w_pool, scale, *, nseq, seq_len, tt, pos0, name):
    t = z.shape[0]
    pool_w = init.shape[-1]
    pg = pool_w // len(POOL_WINDOWS)
    nt = seq_len // tt
    return pl.pallas_call(
        functools.partial(_pool_kernel, tt=tt, pos0=pos0, pg=pg),
        grid=(nseq, nt),
        in_specs=[
            pl.BlockSpec((tt, pool_w), lambda b, i: (b * nt + i, 0)),
            pl.BlockSpec((1, POOL_HALO, pool_w), lambda b, i: (b, 0, 0)),
            pl.BlockSpec((len(POOL_WINDOWS), pg, pg), lambda b, i: (0, 0, 0)),
            pl.BlockSpec((1, pool_w), lambda b, i: (0, 0)),
        ],
        out_specs=pl.BlockSpec((tt, pool_w), lambda b, i: (b * nt + i, 0)),
        out_shape=jax.ShapeDtypeStruct((t, pool_w), BF16),
        scratch_shapes=[pltpu.VMEM((tt + POOL_HALO, pool_w), F32)],
        compiler_params=_cparams("parallel", "arbitrary"),
        name=name,
    )(z, init, w_pool, scale)


def _ssm_param_kernel(are_ref, aim_ref, ldt_ref, bre_ref, bim_ref, pre_ref, pim_ref, bbre_ref, bbim_ref):
    a_re = are_ref[...]
    a_im = aim_ref[...]
    dt = jnp.exp(ldt_ref[...])
    ar = a_re * dt
    ai = a_im * dt
    for p in range(SSM_BLOCK + 1):
        mag = jnp.exp(ar * float(p))
        pre_ref[p] = mag * jnp.cos(ai * float(p))
        pim_ref[p] = mag * jnp.sin(ai * float(p))
    x = pre_ref[1] - 1.0
    y = pim_ref[1]
    den = a_re * a_re + a_im * a_im
    cr = (x * a_re + y * a_im) / den
    ci = (y * a_re - x * a_im) / den
    for c in range(SSM_CH):
        bbre_ref[c] = cr * bre_ref[c] - ci * bim_ref[c]
        bbim_ref[c] = cr * bim_ref[c] + ci * bre_ref[c]


def _ssm_params(a_re, a_im, log_dt, b_re, b_im):
    g, n = a_re.shape
    npow = SSM_BLOCK + 1
    full = lambda *shape: pl.BlockSpec(shape, lambda: (0,) * len(shape))
    return pl.pallas_call(
        _ssm_param_kernel,
        in_specs=[full(g, n), full(g, n), full(g, 1), full(SSM_CH, g, n), full(SSM_CH, g, n)],
        out_specs=[full(npow, g, n), full(npow, g, n), full(SSM_CH, g, n), full(SSM_CH, g, n)],
        out_shape=[
            jax.ShapeDtypeStruct((npow, g, n), F32),
            jax.ShapeDtypeStruct((npow, g, n), F32),
            jax.ShapeDtypeStruct((SSM_CH, g, n), F32),
            jax.ShapeDtypeStruct((SSM_CH, g, n), F32),
        ],
        name="ssm_params",
    )(a_re, a_im, log_dt.reshape(g, 1), jnp.transpose(b_re, (2, 0, 1)), jnp.transpose(b_im, (2, 0, 1)))


def _block_diag(x):
    s, g, r, c = x.shape
    eye = jnp.eye(g, dtype=x.dtype)
    return jnp.einsum("sgrc,gh->sgrhc", x, eye).reshape(s, g * r, g * c)


def _ssm_kernel(z_ref, b_ref, c_ref, pw_ref, d_ref, h0_ref, o_ref, hl_ref, xs_ref, hin_ref, h_ref,
                *, nseq, nb, half, chunk):
    i = pl.program_id(2)
    rows = nb * nseq

    @pl.when(i == 0)
    def _():
        h_ref[...] = h0_ref[0, 0]

    def lane_chunks():
        for k in range(half // chunk):
            yield slice(k * chunk, (k + 1) * chunk), slice(half + k * chunk, half + (k + 1) * chunk)

    offsets_per_chunk = 2
    row_chunks = [(k, slice(k * rows, (k + offsets_per_chunk) * rows))
                  for k in range(0, SSM_BLOCK, offsets_per_chunk)]
    for first, rsl in row_chunks:
        xs_ref[rsl, :] = jnp.dot(z_ref[rsl, :].astype(BF16), b_ref[0], preferred_element_type=F32)
        for l in range(max(first, 1), first + offsets_per_chunk):
            prev = slice((l - 1) * rows, l * rows)
            cur = slice(l * rows, (l + 1) * rows)
            for re, im in lane_chunks():
                lr = pw_ref[0, 1:2, re]
                li = pw_ref[0, 1:2, im]
                pr = xs_ref[prev, re]
                pi = xs_ref[prev, im]
                xs_ref[cur, re] += pr * lr - pi * li
                xs_ref[cur, im] += pr * li + pi * lr

    l8r = pw_ref[0, SSM_BLOCK:SSM_BLOCK + 1, 0:half]
    l8i = pw_ref[0, SSM_BLOCK:SSM_BLOCK + 1, half:2 * half]
    last = (SSM_BLOCK - 1) * rows

    def step(c, carry):
        hr, hi = carry
        r0 = c * nseq
        hin_ref[pl.ds(r0, nseq), 0:half] = hr
        hin_ref[pl.ds(r0, nseq), half:2 * half] = hi
        sr = xs_ref[pl.ds(last + r0, nseq), 0:half]
        si = xs_ref[pl.ds(last + r0, nseq), half:2 * half]
        return l8r * hr - l8i * hi + sr, l8r * hi + l8i * hr + si

    hr, hi = lax.fori_loop(0, nb, step, (h_ref[:, 0:half], h_ref[:, half:2 * half]), unroll=4)
    h_ref[:, 0:half] = hr
    h_ref[:, half:2 * half] = hi

    for first, rsl in row_chunks:
        for l in range(first, first + offsets_per_chunk):
            cur = slice(l * rows, (l + 1) * rows)
            for re, im in lane_chunks():
                lr = pw_ref[0, l + 1:l + 2, re]
                li = pw_ref[0, l + 1:l + 2, im]
                pr = hin_ref[:, re]
                pi = hin_ref[:, im]
                xs_ref[cur, re] += pr * lr - pi * li
                xs_ref[cur, im] += pr * li + pi * lr
        y = jnp.dot(xs_ref[rsl, :].astype(BF16), c_ref[0], preferred_element_type=F32)
        o_ref[rsl, :] = _gelu(y + d_ref[0] * z_ref[rsl, :])

    @pl.when(i == pl.num_programs(2) - 1)
    def _():
        hl_ref[0, 0] = h_ref[...]


def _ssm_mixer(zp, bbd, cbd, pows, dsk, h0, *, ngroups, nseq, nb, name):
    t, ssm_w = zp.shape
    slabs = bbd.shape[0]
    sw = ssm_w // slabs
    half = bbd.shape[2] // 2
    tt = SSM_BLOCK * nb * nseq
    nt = t // (ngroups * tt)
    npow = pows.shape[1]
    kernel = functools.partial(_ssm_kernel, nseq=nseq, nb=nb, half=half, chunk=min(half, 2 * LANES))
    return pl.pallas_call(
        kernel,
        grid=(slabs, ngroups, nt),
        in_specs=[
            pl.BlockSpec((tt, sw), lambda s, b, i: (b * nt + i, s)),
            pl.BlockSpec((1, sw, 2 * half), lambda s, b, i: (s, 0, 0)),
            pl.BlockSpec((1, 2 * half, sw), lambda s, b, i: (s, 0, 0)),
            pl.BlockSpec((1, npow, 2 * half), lambda s, b, i: (s, 0, 0)),
            pl.BlockSpec((1, 1, sw), lambda s, b, i: (s, 0, 0)),
            pl.BlockSpec((1, 1, nseq, 2 * half), lambda s, b, i: (s, b, 0, 0)),
        ],
        out_specs=[
            pl.BlockSpec((tt, sw), lambda s, b, i: (b * nt + i, s)),
            pl.BlockSpec((1, 1, nseq, 2 * half), lambda s, b, i: (s, b, 0, 0)),
        ],
        out_shape=[
            jax.ShapeDtypeStruct((t, ssm_w), F32),
            jax.ShapeDtypeStruct((slabs, ngroups, nseq, 2 * half), F32),
        ],
        scratch_shapes=[
            pltpu.VMEM((tt, 2 * half), F32),
            pltpu.VMEM((nb * nseq, 2 * half), F32),
            pltpu.VMEM((nseq, 2 * half), F32),
        ],
        compiler_params=_cparams("parallel", "arbitrary", "arbitrary"),
        name=name,
    )(zp, bbd, cbd, pows, dsk, h0)


def _glu_kernel(s_ref, w_ref, b_ref, o_ref):
    s = s_ref[...]
    acc = jnp.dot(s.astype(BF16), w_ref[...], preferred_element_type=F32) + b_ref[...]
    o_ref[...] = (s * jax.nn.sigmoid(acc)).astype(o_ref.dtype)


def _glu(s, w, b, *, bm, name):
    t, k = s.shape
    n = w.shape[1]
    return pl.pallas_call(
        _glu_kernel,
        grid=(t // bm,),
        in_specs=[_rows(bm, k), _resident((k, n)), _resident((1, n))],
        out_specs=_rows(bm, n),
        out_shape=jax.ShapeDtypeStruct((t, n), BF16),
        compiler_params=_cparams("parallel"),
        name=name,
    )(s, w, b)


def _outproj_kernel(a_ref, b_ref, x_ref, w_ref, o_ref):
    lhs = jnp.concatenate([a_ref[...], b_ref[...]], axis=1)
    o_ref[...] = x_ref[...] + jnp.dot(lhs, w_ref[...], preferred_element_type=F32)


def _outproj(a, b, x, w, *, bm, name):
    t, ka = a.shape
    kb = b.shape[1]
    n = w.shape[1]
    return pl.pallas_call(
        _outproj_kernel,
        grid=(t // bm,),
        in_specs=[_rows(bm, ka), _rows(bm, kb), _rows(bm, n), _resident((ka + kb, n))],
        out_specs=_rows(bm, n),
        out_shape=jax.ShapeDtypeStruct((t, n), F32),
        compiler_params=_cparams("parallel"),
        name=name,
    )(a, b, x, w)


ROUTE_RANKS = PEER_TOPK + 1
ROUTE_PAIR_COUNTS = tuple(ROUTE_RANKS // (i + 1) for i in range(ROUTE_RANKS))


def _top_rows(x, k):
    tops = []
    for _ in range(k):
        m = jnp.max(x, axis=0, keepdims=True)
        tops.append(m)
        x = jnp.where(x == m, -jnp.inf, x)
    return tops


def _route_kernel(q_ref, k_ref, ra_ref, rb_ref):
    for h in range(PEER_HEADS):
        scores, tops = [], []
        for p in range(2):
            c0 = (2 * h + p) * PEER_DK_HALF
            s = lax.dot_general(k_ref[h, p], q_ref[:, c0:c0 + PEER_DK_HALF], (((1,), (1,)), ((), ())),
                                precision=lax.Precision.HIGHEST, preferred_element_type=F32)
            scores.append(s)
            tops.append(_top_rows(s, ROUTE_RANKS))
        v2 = jnp.concatenate(tops[1], axis=0)
        cand = jnp.concatenate([tops[0][i] + v2[0:n] for i, n in enumerate(ROUTE_PAIR_COUNTS)], axis=0)
        best = _top_rows(cand, ROUTE_RANKS)
        zsum = jnp.zeros_like(best[0])
        for b in best[:PEER_TOPK]:
            zsum = zsum + jnp.exp(b - best[0])
        thr = 0.5 * (best[PEER_TOPK - 1] + best[PEER_TOPK])
        ra_ref[h, 0] = thr - scores[0]
        ra_ref[h, 1] = jnp.exp(scores[0] - tops[0][0]) / zsum
        rb_ref[h, 0] = scores[1]
        rb_ref[h, 1] = jnp.exp(scores[1] - tops[1][0])


def _route(q, keys, *, tt, name):
    t, dq = q.shape
    spec = pl.BlockSpec((PEER_HEADS, 2, PEER_NKEYS, tt), lambda i: (0, 0, 0, i))
    shape = jax.ShapeDtypeStruct((PEER_HEADS, 2, PEER_NKEYS, t), F32)
    return pl.pallas_call(
        _route_kernel,
        grid=(t // tt,),
        in_specs=[
            pl.BlockSpec((tt, dq), lambda i: (i, 0)),
            pl.BlockSpec(keys.shape, lambda i: (0, 0, 0, 0)),
        ],
        out_specs=[spec, spec],
        out_shape=[shape, shape],
        compiler_params=_cparams("parallel"),
        name=name,
    )(q, keys)


PEER_SUB = 2 * PEER_NKEYS


def _peer_kernel(xnt_ref, u_ref, vt_ref, ra_ref, rb_ref, o_ref, *, et):
    @pl.when(pl.program_id(1) == 0)
    def _():
        o_ref[...] = jnp.zeros_like(o_ref)

    ws = []
    for k in range(et // PEER_SUB):
        s_t = jnp.dot(u_ref[k * PEER_SUB:(k + 1) * PEER_SUB, :], xnt_ref[...], preferred_element_type=F32)
        gates = []
        for aa in range(PEER_SUB // PEER_NKEYS):
            a = k * (PEER_SUB // PEER_NKEYS) + aa
            g = None
            for h in range(PEER_HEADS):
                th = ra_ref[h, 0, a:a + 1, :]
                e1 = ra_ref[h, 1, a:a + 1, :]
                term = jnp.where(rb_ref[h, 0] >= th, rb_ref[h, 1], 0.0) * e1
                g = term if g is None else g + term
            gates.append(g)
        ws.append((_gelu(s_t) * jnp.concatenate(gates, axis=0)).astype(BF16))
    o_ref[...] += jnp.dot(vt_ref[0], jnp.concatenate(ws, axis=0), preferred_element_type=F32)


def _peer(xnt, u, vt, ra, rb, *, tt, name):
    d, t = xnt.shape
    nj, _, et = vt.shape
    once = pl.Buffered(1)
    return pl.pallas_call(
        functools.partial(_peer_kernel, et=et),
        grid=(t // tt, nj),
        in_specs=[
            pl.BlockSpec((d, tt), lambda i, j: (0, i), pipeline_mode=once),
            pl.BlockSpec((et, d), lambda i, j: (j, 0)),
            pl.BlockSpec((1, d, et), lambda i, j: (j, 0, 0)),
            pl.BlockSpec((PEER_HEADS, 2, et // PEER_NKEYS, tt), lambda i, j: (0, 0, j, i)),
            pl.BlockSpec((PEER_HEADS, 2, PEER_NKEYS, tt), lambda i, j: (0, 0, 0, i), pipeline_mode=once),
        ],
        out_specs=pl.BlockSpec((d, tt), lambda i, j: (0, i), pipeline_mode=once),
        out_shape=jax.ShapeDtypeStruct((d, t), F32),
        compiler_params=_cparams("parallel", "arbitrary"),
        name=name,
    )(xnt, u, vt, ra, rb)


def _ple_kernel(x1_ref, pt_ref, p_ref, g_ref, wg_ref, wp_ref, gf_ref, o_ref):
    x2 = x1_ref[...] + pt_ref[...].T
    xn = _rmsnorm(x2, g_ref[...]).astype(BF16)
    gate = jax.nn.sigmoid(jnp.dot(xn, wg_ref[...], preferred_element_type=F32))
    ple = jnp.dot(p_ref[...].astype(BF16), wp_ref[...], preferred_element_type=F32)
    o_ref[...] = _rmsnorm(x2 + ple * gate, gf_ref[...])


def _ple(x1, peer_t, p, g_ple, w_gate, w_ple, g_final, *, bm, name):
    t, d = x1.shape
    dp = p.shape[1]
    return pl.pallas_call(
        _ple_kernel,
        grid=(t // bm,),
        in_specs=[
            _rows(bm, d),
            pl.BlockSpec((d, bm), lambda i: (0, i)),
            _rows(bm, dp),
            _resident((1, d)),
            _resident((d, d)),
            _resident((dp, d)),
            _resident((1, d)),
        ],
        out_specs=_rows(bm, d),
        out_shape=jax.ShapeDtypeStruct((t, d), F32),
        compiler_params=_cparams("parallel"),
        name=name,
    )(x1, peer_t, p, g_ple, w_gate, w_ple, g_final)


def _tile(n, pref):
    return pref if n % pref == 0 else n


def _stream(x, p, pool_init, h0_re, h0_im, pos0, wts, *, tag):
    bsz, seq, d = x.shape
    t = bsz * seq
    x2d = x.reshape(t, d)
    mix_w = wts["w_in"].shape[1]
    pool_w = mix_w // 2
    ssm_w = mix_w - pool_w
    slabs, sw, two_half = wts["bbd"].shape
    half = two_half // 2

    z = _norm_matmul(x2d, wts["g_mix"], wts["w_in"], bm=_tile(t, 256), emit_xn=False, name=f"in_proj_{tag}")

    y_pool = _pool_mixer(z, pool_init, wts["w_pool"], wts["pool_scale"], nseq=bsz, seq_len=seq,
                         tt=_tile(seq, 512), pos0=pos0, name=f"pool_{tag}")

    zs = z[:, pool_w:]
    if seq >= 1024:
        nb, nseq, ngroups = 1024 // SSM_BLOCK, 1, bsz
        nt = seq // 1024
        zp = zs.reshape(bsz, nt, nb, SSM_BLOCK, ssm_w).transpose(0, 1, 3, 2, 4).reshape(t, ssm_w)
    else:
        nb, nseq, ngroups = seq // SSM_BLOCK, bsz, 1
        zp = zs.reshape(bsz, nb, SSM_BLOCK, ssm_w).transpose(2, 1, 0, 3).reshape(t, ssm_w)

    def to_slabs(h):
        return h.reshape(ngroups, nseq, slabs, half).transpose(2, 0, 1, 3)

    h0 = jnp.concatenate([to_slabs(h0_re), to_slabs(h0_im)], axis=-1)
    sp, h_last = _ssm_mixer(zp, wts["bbd"], wts["cbd"], wts["pows"], wts["dsk"], h0,
                            ngroups=ngroups, nseq=nseq, nb=nb, name=f"ssm_{tag}")
    sp = _glu(sp, wts["w_glu"], wts["b_glu"], bm=_tile(t, 512), name=f"glu_{tag}")
    if seq >= 1024:
        s = sp.reshape(bsz, nt, SSM_BLOCK, nb, ssm_w).transpose(0, 1, 3, 2, 4).reshape(t, ssm_w)
    else:
        s = sp.reshape(SSM_BLOCK, nb, bsz, ssm_w).transpose(2, 1, 0, 3).reshape(t, ssm_w)

    def from_slabs(h):
        return h.transpose(1, 2, 0, 3).reshape(bsz, slabs * half // SSM_N, SSM_N)

    new_re = from_slabs(h_last[..., :half])
    new_im = from_slabs(h_last[..., half:])

    x1 = _outproj(y_pool, s, x2d, wts["w_out"], bm=_tile(t, 256), name=f"out_proj_{tag}")

    q, xnt = _norm_matmul(x1, wts["g_ffn"], wts["w_query"], bm=_tile(t, 256), emit_xn=True, name=f"query_{tag}")
    tt = _tile(t, 512)
    ra, rb = _route(q, wts["keys"], tt=tt, name=f"route_{tag}")
    peer_t = _peer(xnt, wts["expert_u"], wts["expert_vt"], ra, rb, tt=tt, name=f"peer_{tag}")

    y = _ple(x1, peer_t, p.reshape(t, -1), wts["g_ple"], wts["w_ple_gate"], wts["w_ple"], wts["g_final"],
             bm=_tile(t, 128), name=f"ple_{tag}")

    new_pool = z.reshape(bsz, seq, mix_w)[:, seq - (POOL_HALO - 1):, :pool_w]
    return y.reshape(bsz, seq, d), new_pool, new_re, new_im


def _layer_weights(i, g_mix, w_in, w_pool, pool_scale, ssm_a_re, ssm_a_im, ssm_log_dt, ssm_b_re, ssm_b_im,
                   ssm_c_re, ssm_c_im, ssm_d, w_glu, b_glu, w_out, g_ffn, w_query, peer_sub_keys,
                   expert_u, expert_v, g_ple, w_ple_gate, w_ple, g_final):
    groups, n = ssm_a_re[i].shape
    slabs = groups // SSM_SLAB_GROUPS
    pre, pim, bre, bim = _ssm_params(ssm_a_re[i], ssm_a_im[i], ssm_log_dt[i], ssm_b_re[i], ssm_b_im[i])

    def b_blocks(b):
        return b.reshape(SSM_CH, slabs, SSM_SLAB_GROUPS, n).transpose(1, 2, 0, 3)

    bbd = jnp.concatenate([_block_diag(b_blocks(bre)), _block_diag(b_blocks(bim))], axis=-1).astype(BF16)

    def c_blocks(c):
        return c.reshape(slabs, SSM_SLAB_GROUPS, SSM_CH, n).transpose(0, 1, 3, 2)

    cbd = jnp.concatenate([_block_diag(c_blocks(ssm_c_re[i])), _block_diag(c_blocks(-ssm_c_im[i]))],
                          axis=1).astype(BF16)

    def pow_rows(pw):
        return pw.reshape(pw.shape[0], slabs, SSM_SLAB_GROUPS * n).transpose(1, 0, 2)

    pows = jnp.concatenate([pow_rows(pre), pow_rows(pim)], axis=-1)
    pows = jnp.pad(pows, ((0, 0), (0, 2 * SUBLANES - pows.shape[1]), (0, 0)))
    dsk = ssm_d[i].reshape(slabs, 1, SSM_SLAB_GROUPS * SSM_CH)

    ne, d = expert_v[i].shape
    expert_vt = expert_v[i].reshape(ne // PEER_EXPERT_TILE, PEER_EXPERT_TILE, d).transpose(0, 2, 1).astype(BF16)

    row = lambda v: v.reshape(1, -1)
    return dict(
        g_mix=row(g_mix[i]), w_in=w_in[i].astype(BF16), w_pool=w_pool[i].astype(BF16),
        pool_scale=row(pool_scale[i]), bbd=bbd, cbd=cbd, pows=pows, dsk=dsk,
        w_glu=w_glu[i].astype(BF16), b_glu=row(b_glu[i]), w_out=w_out[i].astype(BF16),
        g_ffn=row(g_ffn[i]), w_query=w_query[i].astype(BF16), keys=peer_sub_keys[i],
        expert_u=expert_u[i].astype(BF16), expert_vt=expert_vt,
        g_ple=row(g_ple[i]), w_ple_gate=w_ple_gate[i].astype(BF16), w_ple=w_ple[i].astype(BF16),
        g_final=row(g_final),
    )


def kernel(x_prompt, x_sample, p_prompt, p_sample, cache_pool, state_ssm_re, state_ssm_im, g_mix, w_in, w_pool, pool_scale, ssm_a_re, ssm_a_im, ssm_log_dt, ssm_b_re, ssm_b_im, ssm_c_re, ssm_c_im, ssm_d, w_glu, b_glu, w_out, g_ffn, w_query, peer_sub_keys, expert_u, expert_v, g_ple, w_ple_gate, w_ple, g_final):
    depth = g_mix.shape[0]
    assert depth == 1, "the final rmsnorm is fused into the layer's last kernel"
    bp = x_prompt.shape[0]
    past_len = 2048
    groups, n = ssm_a_re.shape[1:]
    pool_w = cache_pool.shape[-1]

    wts = _layer_weights(0, g_mix, w_in, w_pool, pool_scale, ssm_a_re, ssm_a_im, ssm_log_dt, ssm_b_re, ssm_b_im,
                         ssm_c_re, ssm_c_im, ssm_d, w_glu, b_glu, w_out, g_ffn, w_query, peer_sub_keys,
                         expert_u, expert_v, g_ple, w_ple_gate, w_ple, g_final)

    zero_pool = jnp.zeros((bp, POOL_HALO, pool_w), F32)
    zero_h = jnp.zeros((bp, groups, n), F32)
    yp, pool_p, re_p, im_p = _stream(x_prompt, p_prompt[0], zero_pool, zero_h, zero_h, 0, wts, tag="prompt")
    pool_init = jnp.pad(cache_pool[0], ((0, 0), (1, 0), (0, 0)))
    ys, pool_s, re_s, im_s = _stream(x_sample, p_sample[0], pool_init, state_ssm_re[0], state_ssm_im[0],
                                     past_len, wts, tag="sample")
    return (yp, ys, pool_p[None], re_p[None], im_p[None], pool_s[None], re_s[None], im_s[None])
```

```python
import functools
import math

import jax
import jax.numpy as jnp
from jax import lax
from jax.experimental import pallas as pl
from jax.experimental.pallas import tpu as pltpu

F32 = jnp.float32
BF16 = jnp.bfloat16
EPS = 1e-6

SUBLANES = 8
LANES = 128
VMEM_LIMIT_BYTES = 60 * 1024 * 1024

POOL_WINDOWS = (2, 4, 8, 16)
POOL_HALO = 16
SSM_CH = 16
SSM_N = 64
SSM_BLOCK = 8
SSM_SLAB_GROUPS = 16
PEER_HEADS = 8
PEER_NKEYS = 128
PEER_DK_HALF = 128
PEER_TOPK = 16
PEER_EXPERT_TILE = 1024


def _cparams(*sem):
    return pltpu.CompilerParams(dimension_semantics=sem, vmem_limit_bytes=VMEM_LIMIT_BYTES)


def _rmsnorm(x, g):
    ms = jnp.mean(x * x, axis=-1, keepdims=True)
    return x * lax.rsqrt(ms + EPS) * g


def _gelu(x):
    return 0.5 * x * (1.0 + lax.erf(x * (1.0 / math.sqrt(2.0))))


def _resident(shape):
    return pl.BlockSpec(shape, lambda i: (0,) * len(shape), pipeline_mode=pl.Buffered(1))


def _rows(bm, n):
    return pl.BlockSpec((bm, n), lambda i: (i, 0))


def _norm_matmul_kernel(x_ref, g_ref, w_ref, o_ref, *rest):
    xn = _rmsnorm(x_ref[...], g_ref[...])
    if rest:
        rest[0][...] = xn.T.astype(BF16)
    o_ref[...] = jnp.dot(xn.astype(BF16), w_ref[...], preferred_element_type=F32)


def _norm_matmul(x, g, w, *, bm, emit_xn, name):
    t, d = x.shape
    n = w.shape[1]
    out_shape = [jax.ShapeDtypeStruct((t, n), F32)]
    out_specs = [_rows(bm, n)]
    if emit_xn:
        out_shape.append(jax.ShapeDtypeStruct((d, t), BF16))
        out_specs.append(pl.BlockSpec((d, bm), lambda i: (0, i)))
    res = pl.pallas_call(
        _norm_matmul_kernel,
        grid=(t // bm,),
        in_specs=[_rows(bm, d), _resident((1, d)), _resident((d, n))],
        out_specs=out_specs,
        out_shape=out_shape,
        compiler_params=_cparams("parallel"),
        name=name,
    )(x, g, w)
    return res if emit_xn else res[0]


def _pool_kernel(z_ref, init_ref, w_ref, sc_ref, o_ref, e_ref, *, tt, pos0, pg):
    i = pl.program_id(1)

    @pl.when(i == 0)
    def _():
        e_ref[0:POOL_HALO, :] = init_ref[0]

    @pl.when(i > 0)
    def _():
        e_ref[0:POOL_HALO, :] = e_ref[tt:tt + POOL_HALO, :]

    e_ref[POOL_HALO:POOL_HALO + tt, :] = z_ref[...]
    pos1 = lax.broadcasted_iota(jnp.int32, (tt, 1), 0) + (i * tt + pos0 + 1)
    for g, w in enumerate(POOL_WINDOWS):
        cols = slice(g * pg, (g + 1) * pg)
        s = e_ref[POOL_HALO:POOL_HALO + tt, cols]
        for back in range(1, w):
            s = s + e_ref[POOL_HALO - back:POOL_HALO - back + tt, cols]
        cnt = jnp.minimum(pos1, w).astype(F32)
        d = s / cnt - z_ref[:, cols]
        y = jnp.dot(d.astype(BF16), w_ref[g], preferred_element_type=F32)
        o_ref[:, cols] = (y * sc_ref[:, cols]).astype(o_ref.dtype)


def _pool_mixer(z, init, w_pool, scale, *, nseq, seq_len, tt, pos0, name):
    t = z.shape[0]
    pool_w = init.shape[-1]
    pg = pool_w // len(POOL_WINDOWS)
    nt = seq_len // tt
    return pl.pallas_call(
        functools.partial(_pool_kernel, tt=tt, pos0=pos0, pg=pg),
        grid=(nseq, nt),
        in_specs=[
            pl.BlockSpec((tt, pool_w), lambda b, i: (b * nt + i, 0)),
            pl.BlockSpec((1, POOL_HALO, pool_w), lambda b, i: (b, 0, 0)),
            pl.BlockSpec((len(POOL_WINDOWS), pg, pg), lambda b, i: (0, 0, 0)),
            pl.BlockSpec((1, pool_w), lambda b, i: (0, 0)),
        ],
        out_specs=pl.BlockSpec((tt, pool_w), lambda b, i: (b * nt + i, 0)),
        out_shape=jax.ShapeDtypeStruct((t, pool_w), BF16),
        scratch_shapes=[pltpu.VMEM((tt + POOL_HALO, pool_w), F32)],
        compiler_params=_cparams("parallel", "arbitrary"),
        name=name,
    )(z, init, w_pool, scale)


def _ssm_param_kernel(are_ref, aim_ref, ldt_ref, bre_ref, bim_ref, pre_ref, pim_ref, bbre_ref, bbim_ref):
    a_re = are_ref[...]
    a_im = aim_ref[...]
    dt = jnp.exp(ldt_ref[...])
    ar = a_re * dt
    ai = a_im * dt
    for p in range(SSM_BLOCK + 1):
        mag = jnp.exp(ar * float(p))
        pre_ref[p] = mag * jnp.cos(ai * float(p))
        pim_ref[p] = mag * jnp.sin(ai * float(p))
    x = pre_ref[1] - 1.0
    y = pim_ref[1]
    den = a_re * a_re + a_im * a_im
    cr = (x * a_re + y * a_im) / den
    ci = (y * a_re - x * a_im) / den
    for c in range(SSM_CH):
        bbre_ref[c] = cr * bre_ref[c] - ci * bim_ref[c]
        bbim_ref[c] = cr * bim_ref[c] + ci * bre_ref[c]


def _ssm_params(a_re, a_im, log_dt, b_re, b_im):
    g, n = a_re.shape
    npow = SSM_BLOCK + 1
    full = lambda *shape: pl.BlockSpec(shape, lambda: (0,) * len(shape))
    return pl.pallas_call(
        _ssm_param_kernel,
        in_specs=[full(g, n), full(g, n), full(g, 1), full(SSM_CH, g, n), full(SSM_CH, g, n)],
        out_specs=[full(npow, g, n), full(npow, g, n), full(SSM_CH, g, n), full(SSM_CH, g, n)],
        out_shape=[
            jax.ShapeDtypeStruct((npow, g, n), F32),
            jax.ShapeDtypeStruct((npow, g, n), F32),
            jax.ShapeDtypeStruct((SSM_CH, g, n), F32),
            jax.ShapeDtypeStruct((SSM_CH, g, n), F32),
        ],
        name="ssm_params",
    )(a_re, a_im, log_dt.reshape(g, 1), jnp.transpose(b_re, (2, 0, 1)), jnp.transpose(b_im, (2, 0, 1)))


def _block_diag(x):
    s, g, r, c = x.shape
    eye = jnp.eye(g, dtype=x.dtype)
    return jnp.einsum("sgrc,gh->sgrhc", x, eye).reshape(s, g * r, g * c)


def _ssm_kernel(z_ref, b_ref, c_ref, pw_ref, d_ref, h0_ref, o_ref, hl_ref, xs_ref, hin_ref, h_ref,
                *, nseq, nb, half, chunk):
    i = pl.program_id(2)
    rows = nb * nseq

    @pl.when(i == 0)
    def _():
        h_ref[...] = h0_ref[0, 0]

    def lane_chunks():
        for k in range(half // chunk):
            yield slice(k * chunk, (k + 1) * chunk), slice(half + k * chunk, half + (k + 1) * chunk)

    offsets_per_chunk = 2
    row_chunks = [(k, slice(k * rows, (k + offsets_per_chunk) * rows))
                  for k in range(0, SSM_BLOCK, offsets_per_chunk)]
    for first, rsl in row_chunks:
        xs_ref[rsl, :] = jnp.dot(z_ref[rsl, :].astype(BF16), b_ref[0], preferred_element_type=F32)
        for l in range(max(first, 1), first + offsets_per_chunk):
            prev = slice((l - 1) * rows, l * rows)
            cur = slice(l * rows, (l + 1) * rows)
            for re, im in lane_chunks():
                lr = pw_ref[0, 1:2, re]
                li = pw_ref[0, 1:2, im]
                pr = xs_ref[prev, re]
                pi = xs_ref[prev, im]
                xs_ref[cur, re] += pr * lr - pi * li
                xs_ref[cur, im] += pr * li + pi * lr

    l8r = pw_ref[0, SSM_BLOCK:SSM_BLOCK + 1, 0:half]
    l8i = pw_ref[0, SSM_BLOCK:SSM_BLOCK + 1, half:2 * half]
    last = (SSM_BLOCK - 1) * rows

    def step(c, carry):
        hr, hi = carry
        r0 = c * nseq
        hin_ref[pl.ds(r0, nseq), 0:half] = hr
        hin_ref[pl.ds(r0, nseq), half:2 * half] = hi
        sr = xs_ref[pl.ds(last + r0, nseq), 0:half]
        si = xs_ref[pl.ds(last + r0, nseq), half:2 * half]
        return l8r * hr - l8i * hi + sr, l8r * hi + l8i * hr + si

    hr, hi = lax.fori_loop(0, nb, step, (h_ref[:, 0:half], h_ref[:, half:2 * half]), unroll=4)
    h_ref[:, 0:half] = hr
    h_ref[:, half:2 * half] = hi

    for first, rsl in row_chunks:
        for l in range(first, first + offsets_per_chunk):
            cur = slice(l * rows, (l + 1) * rows)
            for re, im in lane_chunks():
                lr = pw_ref[0, l + 1:l + 2, re]
                li = pw_ref[0, l + 1:l + 2, im]
                pr = hin_ref[:, re]
                pi = hin_ref[:, im]
                xs_ref[cur, re] += pr * lr - pi * li
                xs_ref[cur, im] += pr * li + pi * lr
        y = jnp.dot(xs_ref[rsl, :].astype(BF16), c_ref[0], preferred_element_type=F32)
        o_ref[rsl, :] = _gelu(y + d_ref[0] * z_ref[rsl, :])

    @pl.when(i == pl.num_programs(2) - 1)
    def _():
        hl_ref[0, 0] = h_ref[...]


def _ssm_mixer(zp, bbd, cbd, pows, dsk, h0, *, ngroups, nseq, nb, name):
    t, ssm_w = zp.shape
    slabs = bbd.shape[0]
    sw = ssm_w // slabs
    half = bbd.shape[2] // 2
    tt = SSM_BLOCK * nb * nseq
    nt = t // (ngroups * tt)
    npow = pows.shape[1]
    kernel = functools.partial(_ssm_kernel, nseq=nseq, nb=nb, half=half, chunk=min(half, 2 * LANES))
    return pl.pallas_call(
        kernel,
        grid=(slabs, ngroups, nt),
        in_specs=[
            pl.BlockSpec((tt, sw), lambda s, b, i: (b * nt + i, s)),
            pl.BlockSpec((1, sw, 2 * half), lambda s, b, i: (s, 0, 0)),
            pl.BlockSpec((1, 2 * half, sw), lambda s, b, i: (s, 0, 0)),
            pl.BlockSpec((1, npow, 2 * half), lambda s, b, i: (s, 0, 0)),
            pl.BlockSpec((1, 1, sw), lambda s, b, i: (s, 0, 0)),
            pl.BlockSpec((1, 1, nseq, 2 * half), lambda s, b, i: (s, b, 0, 0)),
        ],
        out_specs=[
            pl.BlockSpec((tt, sw), lambda s, b, i: (b * nt + i, s)),
            pl.BlockSpec((1, 1, nseq, 2 * half), lambda s, b, i: (s, b, 0, 0)),
        ],
        out_shape=[
            jax.ShapeDtypeStruct((t, ssm_w), F32),
            jax.ShapeDtypeStruct((slabs, ngroups, nseq, 2 * half), F32),
        ],
        scratch_shapes=[
            pltpu.VMEM((tt, 2 * half), F32),
            pltpu.VMEM((nb * nseq, 2 * half), F32),
            pltpu.VMEM((nseq, 2 * half), F32),
        ],
        compiler_params=_cparams("parallel", "arbitrary", "arbitrary"),
        name=name,
    )(zp, bbd, cbd, pows, dsk, h0)


def _glu_kernel(s_ref, w_ref, b_ref, o_ref):
    s = s_ref[...]
    acc = jnp.dot(s.astype(BF16), w_ref[...], preferred_element_type=F32) + b_ref[...]
    o_ref[...] = (s * jax.nn.sigmoid(acc)).astype(o_ref.dtype)


def _glu(s, w, b, *, bm, name):
    t, k = s.shape
    n = w.shape[1]
    return pl.pallas_call(
        _glu_kernel,
        grid=(t // bm,),
        in_specs=[_rows(bm, k), _resident((k, n)), _resident((1, n))],
        out_specs=_rows(bm, n),
        out_shape=jax.ShapeDtypeStruct((t, n), BF16),
        compiler_params=_cparams("parallel"),
        name=name,
    )(s, w, b)


def _outproj_kernel(a_ref, b_ref, x_ref, w_ref, o_ref):
    lhs = jnp.concatenate([a_ref[...], b_ref[...]], axis=1)
    o_ref[...] = x_ref[...] + jnp.dot(lhs, w_ref[...], preferred_element_type=F32)


def _outproj(a, b, x, w, *, bm, name):
    t, ka = a.shape
    kb = b.shape[1]
    n = w.shape[1]
    return pl.pallas_call(
        _outproj_kernel,
        grid=(t // bm,),
        in_specs=[_rows(bm, ka), _rows(bm, kb), _rows(bm, n), _resident((ka + kb, n))],
        out_specs=_rows(bm, n),
        out_shape=jax.ShapeDtypeStruct((t, n), F32),
        compiler_params=_cparams("parallel"),
        name=name,
    )(a, b, x, w)


ROUTE_RANKS = PEER_TOPK + 1
ROUTE_PAIR_COUNTS = tuple(ROUTE_RANKS // (i + 1) for i in range(ROUTE_RANKS))


def _top_rows(x, k, with_rank=False):
    tops = []
    rank = jnp.full(x.shape, float(k), F32) if with_rank else None
    for i in range(k):
        m = jnp.max(x, axis=0, keepdims=True)
        tops.append(m)
        hit = x == m
        if with_rank:
            rank = jnp.where(hit, float(i), rank)
        x = jnp.where(hit, -jnp.inf, x)
    return (tops, rank) if with_rank else tops


def _route_kernel(q_ref, k_ref, ra_ref, rb_ref):
    for h in range(PEER_HEADS):
        def scores(p):
            c0 = (2 * h + p) * PEER_DK_HALF
            return lax.dot_general(k_ref[h, p], q_ref[:, c0:c0 + PEER_DK_HALF], (((1,), (1,)), ((), ())),
                                   precision=lax.Precision.HIGHEST, preferred_element_type=F32)

        s1, s2 = scores(0), scores(1)
        tops1 = _top_rows(s1, ROUTE_RANKS)
        tops2, rank2 = _top_rows(s2, ROUTE_RANKS, with_rank=True)
        v2 = jnp.concatenate(tops2, axis=0)
        cand = jnp.concatenate([tops1[i] + v2[0:n] for i, n in enumerate(ROUTE_PAIR_COUNTS)], axis=0)
        best = _top_rows(cand, ROUTE_RANKS)
        zsum = jnp.zeros_like(best[0])
        for b in best[:PEER_TOPK]:
            zsum = zsum + jnp.exp(b - best[0])
        thr = 0.5 * (best[PEER_TOPK - 1] + best[PEER_TOPK])
        need = thr - s1
        count = jnp.zeros_like(s1)
        for t2 in tops2:
            count = count + jnp.where(t2 >= need, 1.0, 0.0)
        ra_ref[h, 0] = count
        ra_ref[h, 1] = jnp.exp(s1 - tops1[0]) / zsum
        rb_ref[h, 0] = rank2.astype(BF16)
        rb_ref[h, 1] = jnp.exp(s2 - tops2[0]).astype(BF16)


def _route(q, keys, *, tt, name):
    t, dq = q.shape
    spec = pl.BlockSpec((PEER_HEADS, 2, PEER_NKEYS, tt), lambda i: (0, 0, 0, i))
    return pl.pallas_call(
        _route_kernel,
        grid=(t // tt,),
        in_specs=[
            pl.BlockSpec((tt, dq), lambda i: (i, 0)),
            pl.BlockSpec(keys.shape, lambda i: (0, 0, 0, 0)),
        ],
        out_specs=[spec, spec],
        out_shape=[
            jax.ShapeDtypeStruct((PEER_HEADS, 2, PEER_NKEYS, t), F32),
            jax.ShapeDtypeStruct((PEER_HEADS, 2, PEER_NKEYS, t), BF16),
        ],
        compiler_params=_cparams("parallel"),
        name=name,
    )(q, keys)


PEER_SUB = 2 * PEER_NKEYS


def _peer_kernel(xnt_ref, u_ref, vt_ref, ra_ref, rb_ref, o_ref, *, et):
    @pl.when(pl.program_id(1) == 0)
    def _():
        o_ref[...] = jnp.zeros_like(o_ref)

    tt = o_ref.shape[1]
    pack = 2 * SUBLANES
    ws = []
    for k in range(et // PEER_SUB):
        s_t = jnp.dot(u_ref[k * PEER_SUB:(k + 1) * PEER_SUB, :], xnt_ref[...], preferred_element_type=F32)
        gates = []
        for aa in range(PEER_SUB // PEER_NKEYS):
            a = k * (PEER_SUB // PEER_NKEYS) + aa
            acc = [None] * (PEER_NKEYS // pack)
            for h in range(PEER_HEADS):
                count = jnp.broadcast_to(ra_ref[h, 0, a:a + 1, :], (pack, tt)).astype(BF16)
                e1 = jnp.broadcast_to(ra_ref[h, 1, a:a + 1, :], (pack, tt)).astype(BF16)
                for r in range(PEER_NKEYS // pack):
                    rows = slice(r * pack, (r + 1) * pack)
                    term = jnp.where(rb_ref[h, 0, rows, :] < count, rb_ref[h, 1, rows, :], 0.0) * e1
                    acc[r] = term if acc[r] is None else acc[r] + term
            gates.extend(acc)
        ws.append(_gelu(s_t).astype(BF16) * jnp.concatenate(gates, axis=0))
    o_ref[...] += jnp.dot(vt_ref[0], jnp.concatenate(ws, axis=0), preferred_element_type=F32)


def _peer(xnt, u, vt, ra, rb, *, tt, name):
    d, t = xnt.shape
    nj, _, et = vt.shape
    once = pl.Buffered(1)
    return pl.pallas_call(
        functools.partial(_peer_kernel, et=et),
        grid=(t // tt, nj),
        in_specs=[
            pl.BlockSpec((d, tt), lambda i, j: (0, i), pipeline_mode=once),
            pl.BlockSpec((et, d), lambda i, j: (j, 0)),
            pl.BlockSpec((1, d, et), lambda i, j: (j, 0, 0)),
            pl.BlockSpec((PEER_HEADS, 2, et // PEER_NKEYS, tt), lambda i, j: (0, 0, j, i)),
            pl.BlockSpec((PEER_HEADS, 2, PEER_NKEYS, tt), lambda i, j: (0, 0, 0, i), pipeline_mode=once),
        ],
        out_specs=pl.BlockSpec((d, tt), lambda i, j: (0, i), pipeline_mode=once),
        out_shape=jax.ShapeDtypeStruct((d, t), F32),
        compiler_params=_cparams("parallel", "arbitrary"),
        name=name,
    )(xnt, u, vt, ra, rb)


def _ple_kernel(x1_ref, pt_ref, p_ref, g_ref, wg_ref, wp_ref, gf_ref, o_ref):
    x2 = x1_ref[...] + pt_ref[...].T
    xn = _rmsnorm(x2, g_ref[...]).astype(BF16)
    gate = jax.nn.sigmoid(jnp.dot(xn, wg_ref[...], preferred_element_type=F32))
    ple = jnp.dot(p_ref[...].astype(BF16), wp_ref[...], preferred_element_type=F32)
    o_ref[...] = _rmsnorm(x2 + ple * gate, gf_ref[...])


def _ple(x1, peer_t, p, g_ple, w_gate, w_ple, g_final, *, bm, name):
    t, d = x1.shape
    dp = p.shape[1]
    return pl.pallas_call(
        _ple_kernel,
        grid=(t // bm,),
        in_specs=[
            _rows(bm, d),
            pl.BlockSpec((d, bm), lambda i: (0, i)),
            _rows(bm, dp),
            _resident((1, d)),
            _resident((d, d)),
            _resident((dp, d)),
            _resident((1, d)),
        ],
        out_specs=_rows(bm, d),
        out_shape=jax.ShapeDtypeStruct((t, d), F32),
        compiler_params=_cparams("parallel"),
        name=name,
    )(x1, peer_t, p, g_ple, w_gate, w_ple, g_final)


def _tile(n, pref):
    return pref if n % pref == 0 else n


def _stream(x, p, pool_init, h0_re, h0_im, pos0, wts, *, tag):
    bsz, seq, d = x.shape
    t = bsz * seq
    x2d = x.reshape(t, d)
    mix_w = wts["w_in"].shape[1]
    pool_w = mix_w // 2
    ssm_w = mix_w - pool_w
    slabs, sw, two_half = wts["bbd"].shape
    half = two_half // 2

    z = _norm_matmul(x2d, wts["g_mix"], wts["w_in"], bm=_tile(t, 256), emit_xn=False, name=f"in_proj_{tag}")

    y_pool = _pool_mixer(z, pool_init, wts["w_pool"], wts["pool_scale"], nseq=bsz, seq_len=seq,
                         tt=_tile(seq, 512), pos0=pos0, name=f"pool_{tag}")

    zs = z[:, pool_w:]
    if seq >= 1024:
        nb, nseq, ngroups = 1024 // SSM_BLOCK, 1, bsz
        nt = seq // 1024
        zp = zs.reshape(bsz, nt, nb, SSM_BLOCK, ssm_w).transpose(0, 1, 3, 2, 4).reshape(t, ssm_w)
    else:
        nb, nseq, ngroups = seq // SSM_BLOCK, bsz, 1
        zp = zs.reshape(bsz, nb, SSM_BLOCK, ssm_w).transpose(2, 1, 0, 3).reshape(t, ssm_w)

    def to_slabs(h):
        return h.reshape(ngroups, nseq, slabs, half).transpose(2, 0, 1, 3)

    h0 = jnp.concatenate([to_slabs(h0_re), to_slabs(h0_im)], axis=-1)
    sp, h_last = _ssm_mixer(zp, wts["bbd"], wts["cbd"], wts["pows"], wts["dsk"], h0,
                            ngroups=ngroups, nseq=nseq, nb=nb, name=f"ssm_{tag}")
    sp = _glu(sp, wts["w_glu"], wts["b_glu"], bm=_tile(t, 512), name=f"glu_{tag}")
    if seq >= 1024:
        s = sp.reshape(bsz, nt, SSM_BLOCK, nb, ssm_w).transpose(0, 1, 3, 2, 4).reshape(t, ssm_w)
    else:
        s = sp.reshape(SSM_BLOCK, nb, bsz, ssm_w).transpose(2, 1, 0, 3).reshape(t, ssm_w)

    def from_slabs(h):
        return h.transpose(1, 2, 0, 3).reshape(bsz, slabs * half // SSM_N, SSM_N)

    new_re = from_slabs(h_last[..., :half])
    new_im = from_slabs(h_last[..., half:])

    x1 = _outproj(y_pool, s, x2d, wts["w_out"], bm=_tile(t, 256), name=f"out_proj_{tag}")

    q, xnt = _norm_matmul(x1, wts["g_ffn"], wts["w_query"], bm=_tile(t, 256), emit_xn=True, name=f"query_{tag}")
    tt = _tile(t, 512)
    ra, rb = _route(q, wts["keys"], tt=tt, name=f"route_{tag}")
    peer_t = _peer(xnt, wts["expert_u"], wts["expert_vt"], ra, rb, tt=tt, name=f"peer_{tag}")

    y = _ple(x1, peer_t, p.reshape(t, -1), wts["g_ple"], wts["w_ple_gate"], wts["w_ple"], wts["g_final"],
             bm=_tile(t, 128), name=f"ple_{tag}")

    new_pool = z.reshape(bsz, seq, mix_w)[:, seq - (POOL_HALO - 1):, :pool_w]
    return y.reshape(bsz, seq, d), new_pool, new_re, new_im


def _layer_weights(i, g_mix, w_in, w_pool, pool_scale, ssm_a_re, ssm_a_im, ssm_log_dt, ssm_b_re, ssm_b_im,
                   ssm_c_re, ssm_c_im, ssm_d, w_glu, b_glu, w_out, g_ffn, w_query, peer_sub_keys,
                   expert_u, expert_v, g_ple, w_ple_gate, w_ple, g_final):
    groups, n = ssm_a_re[i].shape
    slabs = groups // SSM_SLAB_GROUPS
    pre, pim, bre, bim = _ssm_params(ssm_a_re[i], ssm_a_im[i], ssm_log_dt[i], ssm_b_re[i], ssm_b_im[i])

    def b_blocks(b):
        return b.reshape(SSM_CH, slabs, SSM_SLAB_GROUPS, n).transpose(1, 2, 0, 3)

    bbd = jnp.concatenate([_block_diag(b_blocks(bre)), _block_diag(b_blocks(bim))], axis=-1).astype(BF16)

    def c_blocks(c):
        return c.reshape(slabs, SSM_SLAB_GROUPS, SSM_CH, n).transpose(0, 1, 3, 2)

    cbd = jnp.concatenate([_block_diag(c_blocks(ssm_c_re[i])), _block_diag(c_blocks(-ssm_c_im[i]))],
                          axis=1).astype(BF16)

    def pow_rows(pw):
        return pw.reshape(pw.shape[0], slabs, SSM_SLAB_GROUPS * n).transpose(1, 0, 2)

    pows = jnp.concatenate([pow_rows(pre), pow_rows(pim)], axis=-1)
    pows = jnp.pad(pows, ((0, 0), (0, 2 * SUBLANES - pows.shape[1]), (0, 0)))
    dsk = ssm_d[i].reshape(slabs, 1, SSM_SLAB_GROUPS * SSM_CH)

    ne, d = expert_v[i].shape
    expert_vt = expert_v[i].reshape(ne // PEER_EXPERT_TILE, PEER_EXPERT_TILE, d).transpose(0, 2, 1).astype(BF16)

    row = lambda v: v.reshape(1, -1)
    return dict(
        g_mix=row(g_mix[i]), w_in=w_in[i].astype(BF16), w_pool=w_pool[i].astype(BF16),
        pool_scale=row(pool_scale[i]), bbd=bbd, cbd=cbd, pows=pows, dsk=dsk,
        w_glu=w_glu[i].astype(BF16), b_glu=row(b_glu[i]), w_out=w_out[i].astype(BF16),
        g_ffn=row(g_ffn[i]), w_query=w_query[i].astype(BF16), keys=peer_sub_keys[i],
        expert_u=expert_u[i].astype(BF16), expert_vt=expert_vt,
        g_ple=row(g_ple[i]), w_ple_gate=w_ple_gate[i].astype(BF16), w_ple=w_ple[i].astype(BF16),
        g_final=row(g_final),
    )


def kernel(x_prompt, x_sample, p_prompt, p_sample, cache_pool, state_ssm_re, state_ssm_im, g_mix, w_in, w_pool, pool_scale, ssm_a_re, ssm_a_im, ssm_log_dt, ssm_b_re, ssm_b_im, ssm_c_re, ssm_c_im, ssm_d, w_glu, b_glu, w_out, g_ffn, w_query, peer_sub_keys, expert_u, expert_v, g_ple, w_ple_gate, w_ple, g_final):
    depth = g_mix.shape[0]
    assert depth == 1, "the final rmsnorm is fused into the layer's last kernel"
    bp = x_prompt.shape[0]
    past_len = 2048
    groups, n = ssm_a_re.shape[1:]
    pool_w = cache_pool.shape[-1]

    wts = _layer_weights(0, g_mix, w_in, w_pool, pool_scale, ssm_a_re, ssm_a_im, ssm_log_dt, ssm_b_re, ssm_b_im,
                         ssm_c_re, ssm_c_im, ssm_d, w_glu, b_glu, w_out, g_ffn, w_query, peer_sub_keys,
                         expert_u, expert_v, g_ple, w_ple_gate, w_ple, g_final)

    zero_pool = jnp.zeros((bp, POOL_HALO, pool_w), F32)
    zero_h = jnp.zeros((bp, groups, n), F32)
    yp, pool_p, re_p, im_p = _stream(x_prompt, p_prompt[0], zero_pool, zero_h, zero_h, 0, wts, tag="prompt")
    pool_init = jnp.pad(cache_pool[0], ((0, 0), (1, 0), (0, 0)))
    ys, pool_s, re_s, im_s = _stream(x_sample, p_sample[0], pool_init, state_ssm_re[0], state_ssm_im[0],
                                     past_len, wts, tag="sample")
    return (yp, ys, pool_p[None], re_p[None], im_p[None], pool_s[None], re_s[None], im_s[None])
```

```python
import functools
import math

import jax
import jax.numpy as jnp
from jax import lax
from jax.experimental import pallas as pl
from jax.experimental.pallas import tpu as pltpu

F32 = jnp.float32
BF16 = jnp.bfloat16
EPS = 1e-6

SUBLANES = 8
LANES = 128
VMEM_LIMIT_BYTES = 60 * 1024 * 1024

POOL_WINDOWS = (2, 4, 8, 16)
POOL_HALO = 16
SSM_CH = 16
SSM_N = 64
SSM_BLOCK = 8
SSM_SLAB_GROUPS = 16
PEER_HEADS = 8
PEER_NKEYS = 128
PEER_DK_HALF = 128
PEER_TOPK = 16
PEER_EXPERT_TILE = 1024
PEER_TOKEN_TILE = 512
SAMPLE_PAST_LEN = 2048

PROJ_ROWS = 256
GLU_ROWS = 512
PLE_ROWS = 128
POOL_ROWS = 512
SSM_TILE_TOKENS = 1024


def _cparams(*sem):
    return pltpu.CompilerParams(dimension_semantics=sem, vmem_limit_bytes=VMEM_LIMIT_BYTES)


def _rmsnorm(x, g):
    ms = jnp.mean(x * x, axis=-1, keepdims=True)
    return x * lax.rsqrt(ms + EPS) * g


def _gelu(x):
    return 0.5 * x * (1.0 + lax.erf(x * (1.0 / math.sqrt(2.0))))


def _resident(shape):
    return pl.BlockSpec(shape, lambda i: (0,) * len(shape), pipeline_mode=pl.Buffered(1))


def _rows(bm, n):
    return pl.BlockSpec((bm, n), lambda i: (i, 0))


def _norm_matmul_kernel(x_ref, g_ref, w_ref, o_ref, *rest):
    x = x_ref[...]
    xg = x * g_ref[...]
    inv = lax.rsqrt(jnp.mean(x * x, axis=-1, keepdims=True) + EPS)
    o_ref[...] = inv * jnp.dot(xg.astype(BF16), w_ref[...], preferred_element_type=F32)
    if rest:
        rest[0][...] = (xg * inv).T.astype(BF16)


def _norm_matmul(x, g, w, *, bm, emit_xn, name):
    t, d = x.shape
    n = w.shape[1]
    out_shape = [jax.ShapeDtypeStruct((t, n), F32)]
    out_specs = [_rows(bm, n)]
    if emit_xn:
        out_shape.append(jax.ShapeDtypeStruct((d, t), BF16))
        out_specs.append(pl.BlockSpec((d, bm), lambda i: (0, i)))
    res = pl.pallas_call(
        _norm_matmul_kernel,
        grid=(t // bm,),
        in_specs=[_rows(bm, d), _resident((1, d)), _resident((d, n))],
        out_specs=out_specs,
        out_shape=out_shape,
        compiler_params=_cparams("parallel"),
        name=name,
    )(x, g, w)
    return res if emit_xn else res[0]


def _pool_kernel(z_ref, init_ref, w_ref, sc_ref, o_ref, e_ref, f_ref, g_ref, *, tt, pos0, pg):
    i = pl.program_id(1)
    lead = SUBLANES
    hist = lead + POOL_HALO
    n = hist + tt

    @pl.when(i == 0)
    def _():
        e_ref[0:lead, :] = jnp.zeros((lead, e_ref.shape[1]), F32)
        f_ref[0:lead, :] = jnp.zeros((lead, pg), F32)
        g_ref[0:lead, :] = jnp.zeros((lead, pg), F32)
        e_ref[lead:hist, :] = init_ref[0]

    @pl.when(i > 0)
    def _():
        e_ref[lead:hist, :] = e_ref[n - POOL_HALO:n, :]

    e_ref[hist:n, :] = z_ref[...]
    pos1 = lax.broadcasted_iota(jnp.int32, (tt, 1), 0) + (i * tt + pos0 + 1)
    for g, w in enumerate(POOL_WINDOWS):
        cols = slice(g * pg, (g + 1) * pg)
        src, src_cols, shift = e_ref, cols, 1
        for level in range(w.bit_length() - 1):
            dst = f_ref if level % 2 == 0 else g_ref
            dst[lead:n, :] = src[lead:n, src_cols] + src[lead - shift:n - shift, src_cols]
            src, src_cols, shift = dst, slice(None), 2 * shift
        s = src[hist:n, src_cols]
        cnt = jnp.minimum(pos1, w).astype(F32)
        d = s / cnt - z_ref[:, cols]
        y = jnp.dot(d.astype(BF16), w_ref[g], preferred_element_type=F32)
        o_ref[:, cols] = (y * sc_ref[:, cols]).astype(o_ref.dtype)


def _pool_mixer(z, init, w_pool, scale, *, nseq, seq_len, tt, pos0, name):
    t = z.shape[0]
    pool_w = init.shape[-1]
    pg = pool_w // len(POOL_WINDOWS)
    nt = seq_len // tt
    return pl.pallas_call(
        functools.partial(_pool_kernel, tt=tt, pos0=pos0, pg=pg),
        grid=(nseq, nt),
        in_specs=[
            pl.BlockSpec((tt, pool_w), lambda b, i: (b * nt + i, 0)),
            pl.BlockSpec((1, POOL_HALO, pool_w), lambda b, i: (b, 0, 0)),
            pl.BlockSpec((len(POOL_WINDOWS), pg, pg), lambda b, i: (0, 0, 0)),
            pl.BlockSpec((1, pool_w), lambda b, i: (0, 0)),
        ],
        out_specs=pl.BlockSpec((tt, pool_w), lambda b, i: (b * nt + i, 0)),
        out_shape=jax.ShapeDtypeStruct((t, pool_w), BF16),
        scratch_shapes=[
            pltpu.VMEM((SUBLANES + POOL_HALO + tt, pool_w), F32),
            pltpu.VMEM((SUBLANES + POOL_HALO + tt, pg), F32),
            pltpu.VMEM((SUBLANES + POOL_HALO + tt, pg), F32),
        ],
        compiler_params=_cparams("parallel", "arbitrary"),
        name=name,
    )(z, init, w_pool, scale)


def _ssm_param_kernel(are_ref, aim_ref, ldt_ref, bre_ref, bim_ref, pre_ref, pim_ref, bbre_ref, bbim_ref):
    a_re = are_ref[...]
    a_im = aim_ref[...]
    dt = jnp.exp(ldt_ref[...])
    ar = a_re * dt
    ai = a_im * dt
    for p in range(SSM_BLOCK + 1):
        mag = jnp.exp(ar * float(p))
        pre_ref[p] = mag * jnp.cos(ai * float(p))
        pim_ref[p] = mag * jnp.sin(ai * float(p))
    x = pre_ref[1] - 1.0
    y = pim_ref[1]
    den = a_re * a_re + a_im * a_im
    cr = (x * a_re + y * a_im) / den
    ci = (y * a_re - x * a_im) / den
    for c in range(SSM_CH):
        bbre_ref[c] = cr * bre_ref[c] - ci * bim_ref[c]
        bbim_ref[c] = cr * bim_ref[c] + ci * bre_ref[c]


def _ssm_params(a_re, a_im, log_dt, b_re, b_im):
    g, n = a_re.shape
    npow = SSM_BLOCK + 1
    full = lambda *shape: pl.BlockSpec(shape, lambda: (0,) * len(shape))
    return pl.pallas_call(
        _ssm_param_kernel,
        in_specs=[full(g, n), full(g, n), full(g, 1), full(SSM_CH, g, n), full(SSM_CH, g, n)],
        out_specs=[full(npow, g, n), full(npow, g, n), full(SSM_CH, g, n), full(SSM_CH, g, n)],
        out_shape=[
            jax.ShapeDtypeStruct((npow, g, n), F32),
            jax.ShapeDtypeStruct((npow, g, n), F32),
            jax.ShapeDtypeStruct((SSM_CH, g, n), F32),
            jax.ShapeDtypeStruct((SSM_CH, g, n), F32),
        ],
        name="ssm_params",
    )(a_re, a_im, log_dt.reshape(g, 1), jnp.transpose(b_re, (2, 0, 1)), jnp.transpose(b_im, (2, 0, 1)))


def _block_diag(x):
    s, g, r, c = x.shape
    eye = jnp.eye(g, dtype=x.dtype)
    return jnp.einsum("sgrc,gh->sgrhc", x, eye).reshape(s, g * r, g * c)


def _ssm_kernel(z_ref, b_ref, c_ref, pw_ref, d_ref, h0_ref, o_ref, hl_ref, xs_ref, hin_ref, h_ref, u_ref,
                *, nseq, nb, half, chunk, natural):
    i = pl.program_id(2)
    rows = nb * nseq

    @pl.when(i == 0)
    def _():
        h_ref[...] = h0_ref[0, 0]

    def lane_chunks():
        for k in range(half // chunk):
            yield slice(k * chunk, (k + 1) * chunk), slice(half + k * chunk, half + (k + 1) * chunk)

    offsets_per_chunk = 2
    row_chunks = [(k, slice(k * rows, (k + offsets_per_chunk) * rows))
                  for k in range(0, SSM_BLOCK, offsets_per_chunk)]

    if natural:
        for l in range(SSM_BLOCK):
            u_ref[l * rows:(l + 1) * rows, :] = z_ref[:, l, :]

    def load_u(first, rsl):
        return u_ref[rsl, :] if natural else z_ref[rsl, :]

    def store_y(first, rsl, y):
        if natural:
            for k, l in enumerate(range(first, first + offsets_per_chunk)):
                o_ref[:, l, :] = y[k * rows:(k + 1) * rows, :]
        else:
            o_ref[rsl, :] = y

    for first, rsl in row_chunks:
        xs_ref[rsl, :] = jnp.dot(load_u(first, rsl).astype(BF16), b_ref[0], preferred_element_type=F32)
        for l in range(max(first, 1), first + offsets_per_chunk):
            prev = slice((l - 1) * rows, l * rows)
            cur = slice(l * rows, (l + 1) * rows)
            for re, im in lane_chunks():
                lr = pw_ref[0, 1:2, re]
                li = pw_ref[0, 1:2, im]
                pr = xs_ref[prev, re]
                pi = xs_ref[prev, im]
                xs_ref[cur, re] += pr * lr - pi * li
                xs_ref[cur, im] += pr * li + pi * lr

    l8r = pw_ref[0, SSM_BLOCK:SSM_BLOCK + 1, 0:half]
    l8i = pw_ref[0, SSM_BLOCK:SSM_BLOCK + 1, half:2 * half]
    last = (SSM_BLOCK - 1) * rows

    def step(c, carry):
        hr, hi = carry
        r0 = c * nseq
        hin_ref[pl.ds(r0, nseq), 0:half] = hr
        hin_ref[pl.ds(r0, nseq), half:2 * half] = hi
        sr = xs_ref[pl.ds(last + r0, nseq), 0:half]
        si = xs_ref[pl.ds(last + r0, nseq), half:2 * half]
        return l8r * hr - l8i * hi + sr, l8r * hi + l8i * hr + si

    hr, hi = lax.fori_loop(0, nb, step, (h_ref[:, 0:half], h_ref[:, half:2 * half]), unroll=4)
    h_ref[:, 0:half] = hr
    h_ref[:, half:2 * half] = hi

    for first, rsl in row_chunks:
        for l in range(first, first + offsets_per_chunk):
            cur = slice(l * rows, (l + 1) * rows)
            for re, im in lane_chunks():
                lr = pw_ref[0, l + 1:l + 2, re]
                li = pw_ref[0, l + 1:l + 2, im]
                pr = hin_ref[:, re]
                pi = hin_ref[:, im]
                xs_ref[cur, re] += pr * lr - pi * li
                xs_ref[cur, im] += pr * li + pi * lr
        y = jnp.dot(xs_ref[rsl, :].astype(BF16), c_ref[0], preferred_element_type=F32)
        store_y(first, rsl, _gelu(y + d_ref[0] * load_u(first, rsl)))

    @pl.when(i == pl.num_programs(2) - 1)
    def _():
        hl_ref[0, 0] = h_ref[...]


def _ssm_mixer(z, bbd, cbd, pows, dsk, h0, *, ngroups, nseq, nb, name):
    slabs, sw, two_half = bbd.shape
    half = two_half // 2
    ssm_w = slabs * sw
    t = z.shape[0]
    tt = SSM_BLOCK * nb * nseq
    nt = t // (ngroups * tt)
    npow = pows.shape[1]
    natural = nseq == 1
    if natural:
        first_slab = (z.shape[1] - ssm_w) // sw
        z = z.reshape(t // SSM_BLOCK, SSM_BLOCK, z.shape[1])
        z_spec = pl.BlockSpec((nb, SSM_BLOCK, sw), lambda s, b, i: (b * nt + i, 0, first_slab + s))
        o_spec = pl.BlockSpec((nb, SSM_BLOCK, sw), lambda s, b, i: (b * nt + i, 0, s))
        o_shape = jax.ShapeDtypeStruct((t // SSM_BLOCK, SSM_BLOCK, ssm_w), F32)
    else:
        z_spec = o_spec = pl.BlockSpec((tt, sw), lambda s, b, i: (b * nt + i, s))
        o_shape = jax.ShapeDtypeStruct((t, ssm_w), F32)
    kernel = functools.partial(_ssm_kernel, nseq=nseq, nb=nb, half=half, chunk=min(half, 2 * LANES),
                               natural=natural)
    y, h_last = pl.pallas_call(
        kernel,
        grid=(slabs, ngroups, nt),
        in_specs=[
            z_spec,
            pl.BlockSpec((1, sw, 2 * half), lambda s, b, i: (s, 0, 0)),
            pl.BlockSpec((1, 2 * half, sw), lambda s, b, i: (s, 0, 0)),
            pl.BlockSpec((1, npow, 2 * half), lambda s, b, i: (s, 0, 0)),
            pl.BlockSpec((1, 1, sw), lambda s, b, i: (s, 0, 0)),
            pl.BlockSpec((1, 1, nseq, 2 * half), lambda s, b, i: (s, b, 0, 0)),
        ],
        out_specs=[
            o_spec,
            pl.BlockSpec((1, 1, nseq, 2 * half), lambda s, b, i: (s, b, 0, 0)),
        ],
        out_shape=[
            o_shape,
            jax.ShapeDtypeStruct((slabs, ngroups, nseq, 2 * half), F32),
        ],
        scratch_shapes=[
            pltpu.VMEM((tt, 2 * half), F32),
            pltpu.VMEM((nb * nseq, 2 * half), F32),
            pltpu.VMEM((nseq, 2 * half), F32),
            pltpu.VMEM((tt, sw) if natural else (SUBLANES, LANES), F32),
        ],
        compiler_params=_cparams("parallel", "arbitrary", "arbitrary"),
        name=name,
    )(z, bbd, cbd, pows, dsk, h0)
    return y.reshape(t, ssm_w), h_last


def _glu_kernel(s_ref, w_ref, b_ref, o_ref):
    s = s_ref[...]
    acc = jnp.dot(s.astype(BF16), w_ref[...], preferred_element_type=F32) + b_ref[...]
    o_ref[...] = (s * jax.nn.sigmoid(acc)).astype(o_ref.dtype)


def _glu(s, w, b, *, bm, name):
    t, k = s.shape
    n = w.shape[1]
    return pl.pallas_call(
        _glu_kernel,
        grid=(t // bm,),
        in_specs=[_rows(bm, k), _resident((k, n)), _resident((1, n))],
        out_specs=_rows(bm, n),
        out_shape=jax.ShapeDtypeStruct((t, n), BF16),
        compiler_params=_cparams("parallel"),
        name=name,
    )(s, w, b)


def _outproj_kernel(a_ref, b_ref, x_ref, w_ref, o_ref):
    lhs = jnp.concatenate([a_ref[...], b_ref[...]], axis=1)
    o_ref[...] = x_ref[...] + jnp.dot(lhs, w_ref[...], preferred_element_type=F32)


def _outproj(a, b, x, w, *, bm, name):
    t, ka = a.shape
    kb = b.shape[1]
    n = w.shape[1]
    return pl.pallas_call(
        _outproj_kernel,
        grid=(t // bm,),
        in_specs=[_rows(bm, ka), _rows(bm, kb), _rows(bm, n), _resident((ka + kb, n))],
        out_specs=_rows(bm, n),
        out_shape=jax.ShapeDtypeStruct((t, n), F32),
        compiler_params=_cparams("parallel"),
        name=name,
    )(a, b, x, w)


ROUTE_RANKS = PEER_TOPK + 1
ROUTE_PAIR_COUNTS = tuple(ROUTE_RANKS // (i + 1) for i in range(ROUTE_RANKS))


def _top_rows(x, k, with_rank=False):
    tops = []
    rank = jnp.full(x.shape, float(k), F32) if with_rank else None
    for i in range(k):
        m = jnp.max(x, axis=0, keepdims=True)
        tops.append(m)
        hit = x == m
        if with_rank:
            rank = jnp.where(hit, float(i), rank)
        x = jnp.where(hit, -jnp.inf, x)
    return (tops, rank) if with_rank else tops


def _route_kernel(q_ref, k_ref, ra_ref, rb_ref):
    for h in range(PEER_HEADS):
        def scores(p):
            c0 = (2 * h + p) * PEER_DK_HALF
            return lax.dot_general(k_ref[h, p], q_ref[:, c0:c0 + PEER_DK_HALF], (((1,), (1,)), ((), ())),
                                   precision=lax.Precision.HIGHEST, preferred_element_type=F32)

        s1, s2 = scores(0), scores(1)
        tops1 = _top_rows(s1, ROUTE_RANKS)
        tops2, rank2 = _top_rows(s2, ROUTE_RANKS, with_rank=True)
        v2 = jnp.concatenate(tops2, axis=0)
        cand = jnp.concatenate([tops1[i] + v2[0:n] for i, n in enumerate(ROUTE_PAIR_COUNTS)], axis=0)
        best = _top_rows(cand, ROUTE_RANKS)
        zsum = jnp.zeros_like(best[0])
        for b in best[:PEER_TOPK]:
            zsum = zsum + jnp.exp(b - best[0])
        thr = 0.5 * (best[PEER_TOPK - 1] + best[PEER_TOPK])
        need = thr - s1
        count = jnp.zeros_like(s1)
        for t2 in tops2[:PEER_TOPK // 2]:
            count = count + jnp.where(t2 >= need, 1.0, 0.0)
        need_best = thr - tops1[0]
        count_best = jnp.zeros_like(need_best)
        for t2 in tops2:
            count_best = count_best + jnp.where(t2 >= need_best, 1.0, 0.0)
        ra_ref[h, 0] = jnp.where(s1 == tops1[0], count_best, count)
        ra_ref[h, 1] = jnp.exp(s1 - tops1[0]) / zsum
        rb_ref[h, 0] = rank2.astype(BF16)
        rb_ref[h, 1] = jnp.exp(s2 - tops2[0]).astype(BF16)


def _route(q, keys, *, tt, name):
    t, dq = q.shape
    spec = pl.BlockSpec((PEER_HEADS, 2, PEER_NKEYS, tt), lambda i: (0, 0, 0, i))
    return pl.pallas_call(
        _route_kernel,
        grid=(t // tt,),
        in_specs=[
            pl.BlockSpec((tt, dq), lambda i: (i, 0)),
            pl.BlockSpec(keys.shape, lambda i: (0, 0, 0, 0)),
        ],
        out_specs=[spec, spec],
        out_shape=[
            jax.ShapeDtypeStruct((PEER_HEADS, 2, PEER_NKEYS, t), F32),
            jax.ShapeDtypeStruct((PEER_HEADS, 2, PEER_NKEYS, t), BF16),
        ],
        compiler_params=_cparams("parallel"),
        name=name,
    )(q, keys)


PEER_SUB = 2 * PEER_NKEYS


def _peer_kernel(xnt_ref, u_ref, vt_ref, ra_ref, rb_ref, o_ref, *, et):
    @pl.when(pl.program_id(1) == 0)
    def _():
        o_ref[...] = jnp.zeros_like(o_ref)

    tt = o_ref.shape[1]
    pack = 2 * SUBLANES
    ws = []
    for k in range(et // PEER_SUB):
        s_t = jnp.dot(u_ref[k * PEER_SUB:(k + 1) * PEER_SUB, :], xnt_ref[...], preferred_element_type=F32)
        gates = []
        for aa in range(PEER_SUB // PEER_NKEYS):
            a = k * (PEER_SUB // PEER_NKEYS) + aa
            acc = [None] * (PEER_NKEYS // pack)
            for h in range(PEER_HEADS):
                count = jnp.broadcast_to(ra_ref[h, 0, a:a + 1, :], (pack, tt)).astype(BF16)
                e1 = jnp.broadcast_to(ra_ref[h, 1, a:a + 1, :], (pack, tt)).astype(BF16)
                for r in range(PEER_NKEYS // pack):
                    rows = slice(r * pack, (r + 1) * pack)
                    term = jnp.where(rb_ref[h, 0, rows, :] < count, rb_ref[h, 1, rows, :], 0.0) * e1
                    acc[r] = term if acc[r] is None else acc[r] + term
            gates.extend(acc)
        ws.append(_gelu(s_t).astype(BF16) * jnp.concatenate(gates, axis=0))
    o_ref[...] += jnp.dot(vt_ref[0], jnp.concatenate(ws, axis=0), preferred_element_type=F32)


def _peer(xnt, u, vt, ra, rb, *, tt, name):
    d, t = xnt.shape
    nj, _, et = vt.shape
    once = pl.Buffered(1)
    return pl.pallas_call(
        functools.partial(_peer_kernel, et=et),
        grid=(t // tt, nj),
        in_specs=[
            pl.BlockSpec((d, tt), lambda i, j: (0, i), pipeline_mode=once),
            pl.BlockSpec((et, d), lambda i, j: (j, 0)),
            pl.BlockSpec((1, d, et), lambda i, j: (j, 0, 0)),
            pl.BlockSpec((PEER_HEADS, 2, et // PEER_NKEYS, tt), lambda i, j: (0, 0, j, i)),
            pl.BlockSpec((PEER_HEADS, 2, PEER_NKEYS, tt), lambda i, j: (0, 0, 0, i), pipeline_mode=once),
        ],
        out_specs=pl.BlockSpec((d, tt), lambda i, j: (0, i), pipeline_mode=once),
        out_shape=jax.ShapeDtypeStruct((d, t), F32),
        compiler_params=_cparams("parallel", "arbitrary"),
        name=name,
    )(xnt, u, vt, ra, rb)


def _ple_kernel(x1_ref, pt_ref, p_ref, g_ref, wg_ref, wp_ref, gf_ref, o_ref):
    x2 = x1_ref[...] + pt_ref[...].T
    inv = lax.rsqrt(jnp.mean(x2 * x2, axis=-1, keepdims=True) + EPS)
    pre = jnp.dot((x2 * g_ref[...]).astype(BF16), wg_ref[...], preferred_element_type=F32)
    gate = jax.nn.sigmoid(inv * pre)
    ple = jnp.dot(p_ref[...].astype(BF16), wp_ref[...], preferred_element_type=F32)
    o_ref[...] = _rmsnorm(x2 + ple * gate, gf_ref[...])


def _ple(x1, peer_t, p, g_ple, w_gate, w_ple, g_final, *, bm, name):
    t, d = x1.shape
    dp = p.shape[1]
    return pl.pallas_call(
        _ple_kernel,
        grid=(t // bm,),
        in_specs=[
            _rows(bm, d),
            pl.BlockSpec((d, bm), lambda i: (0, i)),
            _rows(bm, dp),
            _resident((1, d)),
            _resident((d, d)),
            _resident((dp, d)),
            _resident((1, d)),
        ],
        out_specs=_rows(bm, d),
        out_shape=jax.ShapeDtypeStruct((t, d), F32),
        compiler_params=_cparams("parallel"),
        name=name,
    )(x1, peer_t, p, g_ple, w_gate, w_ple, g_final)


def _tile(n, pref):
    return pref if n % pref == 0 else n


def _stream(x, p, pool_init, h0_re, h0_im, pos0, wts, *, tag):
    bsz, seq, d = x.shape
    t = bsz * seq
    x2d = x.reshape(t, d)
    mix_w = wts["w_in"].shape[1]
    pool_w = mix_w // 2
    ssm_w = mix_w - pool_w
    slabs, sw, two_half = wts["bbd"].shape
    half = two_half // 2

    z = _norm_matmul(x2d, wts["g_mix"], wts["w_in"], bm=_tile(t, PROJ_ROWS), emit_xn=False, name=f"in_proj_{tag}")

    y_pool = _pool_mixer(z, pool_init, wts["w_pool"], wts["pool_scale"], nseq=bsz, seq_len=seq,
                         tt=_tile(seq, POOL_ROWS), pos0=pos0, name=f"pool_{tag}")

    long_seq = seq >= SSM_TILE_TOKENS
    if long_seq:
        nb, nseq, ngroups = SSM_TILE_TOKENS // SSM_BLOCK, 1, bsz
        zp = z
    else:
        nb, nseq, ngroups = seq // SSM_BLOCK, bsz, 1
        zp = z[:, pool_w:].reshape(bsz, nb, SSM_BLOCK, ssm_w).transpose(2, 1, 0, 3).reshape(t, ssm_w)

    def to_slabs(h):
        return h.reshape(ngroups, nseq, slabs, half).transpose(2, 0, 1, 3)

    h0 = jnp.concatenate([to_slabs(h0_re), to_slabs(h0_im)], axis=-1)
    sp, h_last = _ssm_mixer(zp, wts["bbd"], wts["cbd"], wts["pows"], wts["dsk"], h0,
                            ngroups=ngroups, nseq=nseq, nb=nb, name=f"ssm_{tag}")
    sp = _glu(sp, wts["w_glu"], wts["b_glu"], bm=_tile(t, GLU_ROWS), name=f"glu_{tag}")
    s = sp if long_seq else sp.reshape(SSM_BLOCK, nb, bsz, ssm_w).transpose(2, 1, 0, 3).reshape(t, ssm_w)

    def from_slabs(h):
        return h.transpose(1, 2, 0, 3).reshape(bsz, slabs * half // SSM_N, SSM_N)

    new_re = from_slabs(h_last[..., :half])
    new_im = from_slabs(h_last[..., half:])

    x1 = _outproj(y_pool, s, x2d, wts["w_out"], bm=_tile(t, PROJ_ROWS), name=f"out_proj_{tag}")

    q, xnt = _norm_matmul(x1, wts["g_ffn"], wts["w_query"], bm=_tile(t, PROJ_ROWS), emit_xn=True, name=f"query_{tag}")
    tt = _tile(t, PEER_TOKEN_TILE)
    ra, rb = _route(q, wts["keys"], tt=tt, name=f"route_{tag}")
    peer_t = _peer(xnt, wts["expert_u"], wts["expert_vt"], ra, rb, tt=tt, name=f"peer_{tag}")

    y = _ple(x1, peer_t, p.reshape(t, -1), wts["g_ple"], wts["w_ple_gate"], wts["w_ple"], wts["g_final"],
             bm=_tile(t, PLE_ROWS), name=f"ple_{tag}")

    new_pool = z.reshape(bsz, seq, mix_w)[:, seq - (POOL_HALO - 1):, :pool_w]
    return y.reshape(bsz, seq, d), new_pool, new_re, new_im


def _layer_weights(i, g_mix, w_in, w_pool, pool_scale, ssm_a_re, ssm_a_im, ssm_log_dt, ssm_b_re, ssm_b_im,
                   ssm_c_re, ssm_c_im, ssm_d, w_glu, b_glu, w_out, g_ffn, w_query, peer_sub_keys,
                   expert_u, expert_v, g_ple, w_ple_gate, w_ple, g_final):
    groups, n = ssm_a_re[i].shape
    slabs = groups // SSM_SLAB_GROUPS
    pre, pim, bre, bim = _ssm_params(ssm_a_re[i], ssm_a_im[i], ssm_log_dt[i], ssm_b_re[i], ssm_b_im[i])

    def b_blocks(b):
        return b.reshape(SSM_CH, slabs, SSM_SLAB_GROUPS, n).transpose(1, 2, 0, 3)

    bbd = jnp.concatenate([_block_diag(b_blocks(bre)), _block_diag(b_blocks(bim))], axis=-1).astype(BF16)

    def c_blocks(c):
        return c.reshape(slabs, SSM_SLAB_GROUPS, SSM_CH, n).transpose(0, 1, 3, 2)

    cbd = jnp.concatenate([_block_diag(c_blocks(ssm_c_re[i])), _block_diag(c_blocks(-ssm_c_im[i]))],
                          axis=1).astype(BF16)

    def pow_rows(pw):
        return pw.reshape(pw.shape[0], slabs, SSM_SLAB_GROUPS * n).transpose(1, 0, 2)

    pows = jnp.concatenate([pow_rows(pre), pow_rows(pim)], axis=-1)
    pows = jnp.pad(pows, ((0, 0), (0, 2 * SUBLANES - pows.shape[1]), (0, 0)))
    dsk = ssm_d[i].reshape(slabs, 1, SSM_SLAB_GROUPS * SSM_CH)

    ne, d = expert_v[i].shape
    expert_vt = expert_v[i].reshape(ne // PEER_EXPERT_TILE, PEER_EXPERT_TILE, d).transpose(0, 2, 1).astype(BF16)

    row = lambda v: v.reshape(1, -1)
    return dict(
        g_mix=row(g_mix[i]), w_in=w_in[i].astype(BF16), w_pool=w_pool[i].astype(BF16),
        pool_scale=row(pool_scale[i]), bbd=bbd, cbd=cbd, pows=pows, dsk=dsk,
        w_glu=w_glu[i].astype(BF16), b_glu=row(b_glu[i]), w_out=w_out[i].astype(BF16),
        g_ffn=row(g_ffn[i]), w_query=w_query[i].astype(BF16), keys=peer_sub_keys[i],
        expert_u=expert_u[i].astype(BF16), expert_vt=expert_vt,
        g_ple=row(g_ple[i]), w_ple_gate=w_ple_gate[i].astype(BF16), w_ple=w_ple[i].astype(BF16),
        g_final=row(g_final),
    )


def kernel(x_prompt, x_sample, p_prompt, p_sample, cache_pool, state_ssm_re, state_ssm_im, g_mix, w_in, w_pool, pool_scale, ssm_a_re, ssm_a_im, ssm_log_dt, ssm_b_re, ssm_b_im, ssm_c_re, ssm_c_im, ssm_d, w_glu, b_glu, w_out, g_ffn, w_query, peer_sub_keys, expert_u, expert_v, g_ple, w_ple_gate, w_ple, g_final):
    depth = g_mix.shape[0]
    assert depth == 1, "the final rmsnorm is fused into the layer's last kernel"
    bp = x_prompt.shape[0]
    groups, n = ssm_a_re.shape[1:]
    pool_w = cache_pool.shape[-1]

    wts = _layer_weights(0, g_mix, w_in, w_pool, pool_scale, ssm_a_re, ssm_a_im, ssm_log_dt, ssm_b_re, ssm_b_im,
                         ssm_c_re, ssm_c_im, ssm_d, w_glu, b_glu, w_out, g_ffn, w_query, peer_sub_keys,
                         expert_u, expert_v, g_ple, w_ple_gate, w_ple, g_final)

    zero_pool = jnp.zeros((bp, POOL_HALO, pool_w), F32)
    zero_h = jnp.zeros((bp, groups, n), F32)
    yp, pool_p, re_p, im_p = _stream(x_prompt, p_prompt[0], zero_pool, zero_h, zero_h, 0, wts, tag="prompt")
    pool_init = jnp.pad(cache_pool[0], ((0, 0), (1, 0), (0, 0)))
    ys, pool_s, re_s, im_s = _stream(x_sample, p_sample[0], pool_init, state_ssm_re[0], state_ssm_im[0],
                                     SAMPLE_PAST_LEN, wts, tag="sample")
    return (yp, ys, pool_p[None], re_p[None], im_p[None], pool_s[None], re_s[None], im_s[None])
```

```python
import functools
import math

import jax
import jax.numpy as jnp
from jax import lax
from jax.experimental import pallas as pl
from jax.experimental.pallas import tpu as pltpu

F32 = jnp.float32
BF16 = jnp.bfloat16
EPS = 1e-6

SUBLANES = 8
LANES = 128
VMEM_LIMIT_BYTES = 63 * 1024 * 1024

POOL_WINDOWS = (2, 4, 8, 16)
POOL_HALO = 16
SSM_CH = 16
SSM_N = 64
SSM_BLOCK = 8
SSM_SLAB_GROUPS = 16
PEER_HEADS = 8
PEER_NKEYS = 128
PEER_DK_HALF = 128
PEER_TOPK = 16
PEER_EXPERT_TILE = 1024
PEER_TOKEN_TILE = 512
SAMPLE_PAST_LEN = 2048

PROJ_ROWS = 256
GLU_ROWS = 512
PLE_ROWS = 128
POOL_ROWS = 512
SSM_TILE_TOKENS = 1024


def _cparams(*sem):
    return pltpu.CompilerParams(dimension_semantics=sem, vmem_limit_bytes=VMEM_LIMIT_BYTES)


def _rmsnorm(x, g):
    ms = jnp.mean(x * x, axis=-1, keepdims=True)
    return x * lax.rsqrt(ms + EPS) * g


def _gelu(x):
    return 0.5 * x * (1.0 + lax.erf(x * (1.0 / math.sqrt(2.0))))


def _resident(shape):
    return pl.BlockSpec(shape, lambda i: (0,) * len(shape), pipeline_mode=pl.Buffered(1))


def _rows(bm, n):
    return pl.BlockSpec((bm, n), lambda i: (i, 0))


def _norm_matmul_kernel(x_ref, g_ref, w_ref, o_ref, *rest):
    x = x_ref[...]
    xg = x * g_ref[...]
    inv = lax.rsqrt(jnp.mean(x * x, axis=-1, keepdims=True) + EPS)
    o_ref[...] = inv * jnp.dot(xg.astype(BF16), w_ref[...], preferred_element_type=F32)
    if rest:
        rest[0][...] = (xg * inv).T.astype(BF16)


def _norm_matmul(x, g, w, *, bm, emit_xn, name):
    t, d = x.shape
    n = w.shape[1]
    out_shape = [jax.ShapeDtypeStruct((t, n), F32)]
    out_specs = [_rows(bm, n)]
    if emit_xn:
        out_shape.append(jax.ShapeDtypeStruct((d, t), BF16))
        out_specs.append(pl.BlockSpec((d, bm), lambda i: (0, i)))
    res = pl.pallas_call(
        _norm_matmul_kernel,
        grid=(t // bm,),
        in_specs=[_rows(bm, d), _resident((1, d)), _resident((d, n))],
        out_specs=out_specs,
        out_shape=out_shape,
        compiler_params=_cparams("parallel"),
        name=name,
    )(x, g, w)
    return res if emit_xn else res[0]


def _pool_kernel(z_ref, init_ref, w_ref, sc_ref, o_ref, e_ref, f_ref, g_ref, *, tt, pos0, pg):
    i = pl.program_id(1)
    lead = SUBLANES
    hist = lead + POOL_HALO
    n = hist + tt

    @pl.when(i == 0)
    def _():
        e_ref[0:lead, :] = jnp.zeros((lead, e_ref.shape[1]), F32)
        f_ref[0:lead, :] = jnp.zeros((lead, pg), F32)
        g_ref[0:lead, :] = jnp.zeros((lead, pg), F32)
        e_ref[lead:hist, :] = init_ref[0]

    @pl.when(i > 0)
    def _():
        e_ref[lead:hist, :] = e_ref[n - POOL_HALO:n, :]

    e_ref[hist:n, :] = z_ref[...]
    pos1 = lax.broadcasted_iota(jnp.int32, (tt, 1), 0) + (i * tt + pos0 + 1)
    for g, w in enumerate(POOL_WINDOWS):
        cols = slice(g * pg, (g + 1) * pg)
        src, src_cols, shift = e_ref, cols, 1
        for level in range(w.bit_length() - 1):
            dst = f_ref if level % 2 == 0 else g_ref
            dst[lead:n, :] = src[lead:n, src_cols] + src[lead - shift:n - shift, src_cols]
            src, src_cols, shift = dst, slice(None), 2 * shift
        s = src[hist:n, src_cols]
        cnt = jnp.minimum(pos1, w).astype(F32)
        d = s / cnt - z_ref[:, cols]
        y = jnp.dot(d.astype(BF16), w_ref[g], preferred_element_type=F32)
        o_ref[:, cols] = (y * sc_ref[:, cols]).astype(o_ref.dtype)


def _pool_mixer(z, init, w_pool, scale, *, nseq, seq_len, tt, pos0, name):
    t = z.shape[0]
    pool_w = init.shape[-1]
    pg = pool_w // len(POOL_WINDOWS)
    nt = seq_len // tt
    return pl.pallas_call(
        functools.partial(_pool_kernel, tt=tt, pos0=pos0, pg=pg),
        grid=(nseq, nt),
        in_specs=[
            pl.BlockSpec((tt, pool_w), lambda b, i: (b * nt + i, 0)),
            pl.BlockSpec((1, POOL_HALO, pool_w), lambda b, i: (b, 0, 0)),
            pl.BlockSpec((len(POOL_WINDOWS), pg, pg), lambda b, i: (0, 0, 0)),
            pl.BlockSpec((1, pool_w), lambda b, i: (0, 0)),
        ],
        out_specs=pl.BlockSpec((tt, pool_w), lambda b, i: (b * nt + i, 0)),
        out_shape=jax.ShapeDtypeStruct((t, pool_w), BF16),
        scratch_shapes=[
            pltpu.VMEM((SUBLANES + POOL_HALO + tt, pool_w), F32),
            pltpu.VMEM((SUBLANES + POOL_HALO + tt, pg), F32),
            pltpu.VMEM((SUBLANES + POOL_HALO + tt, pg), F32),
        ],
        compiler_params=_cparams("parallel", "arbitrary"),
        name=name,
    )(z, init, w_pool, scale)


def _ssm_param_kernel(are_ref, aim_ref, ldt_ref, bre_ref, bim_ref, pre_ref, pim_ref, bbre_ref, bbim_ref):
    a_re = are_ref[...]
    a_im = aim_ref[...]
    dt = jnp.exp(ldt_ref[...])
    ar = a_re * dt
    ai = a_im * dt
    for p in range(SSM_BLOCK + 1):
        mag = jnp.exp(ar * float(p))
        pre_ref[p] = mag * jnp.cos(ai * float(p))
        pim_ref[p] = mag * jnp.sin(ai * float(p))
    x = pre_ref[1] - 1.0
    y = pim_ref[1]
    den = a_re * a_re + a_im * a_im
    cr = (x * a_re + y * a_im) / den
    ci = (y * a_re - x * a_im) / den
    for c in range(SSM_CH):
        bbre_ref[c] = cr * bre_ref[c] - ci * bim_ref[c]
        bbim_ref[c] = cr * bim_ref[c] + ci * bre_ref[c]


def _ssm_params(a_re, a_im, log_dt, b_re, b_im):
    g, n = a_re.shape
    npow = SSM_BLOCK + 1
    full = lambda *shape: pl.BlockSpec(shape, lambda: (0,) * len(shape))
    return pl.pallas_call(
        _ssm_param_kernel,
        in_specs=[full(g, n), full(g, n), full(g, 1), full(SSM_CH, g, n), full(SSM_CH, g, n)],
        out_specs=[full(npow, g, n), full(npow, g, n), full(SSM_CH, g, n), full(SSM_CH, g, n)],
        out_shape=[
            jax.ShapeDtypeStruct((npow, g, n), F32),
            jax.ShapeDtypeStruct((npow, g, n), F32),
            jax.ShapeDtypeStruct((SSM_CH, g, n), F32),
            jax.ShapeDtypeStruct((SSM_CH, g, n), F32),
        ],
        name="ssm_params",
    )(a_re, a_im, log_dt.reshape(g, 1), jnp.transpose(b_re, (2, 0, 1)), jnp.transpose(b_im, (2, 0, 1)))


def _block_diag(x):
    s, g, r, c = x.shape
    eye = jnp.eye(g, dtype=x.dtype)
    return jnp.einsum("sgrc,gh->sgrhc", x, eye).reshape(s, g * r, g * c)


def _ssm_kernel(z_ref, b_ref, c_ref, pw_ref, d_ref, h0_ref, o_ref, hl_ref, xs_ref, hin_ref, h_ref, u_ref,
                *, nseq, nb, half, chunk, natural):
    i = pl.program_id(2)
    rows = nb * nseq

    @pl.when(i == 0)
    def _():
        h_ref[...] = h0_ref[0, 0]

    def lane_chunks():
        for k in range(half // chunk):
            yield slice(k * chunk, (k + 1) * chunk), slice(half + k * chunk, half + (k + 1) * chunk)

    offsets_per_chunk = 2
    row_chunks = [(k, slice(k * rows, (k + offsets_per_chunk) * rows))
                  for k in range(0, SSM_BLOCK, offsets_per_chunk)]

    if natural:
        for l in range(SSM_BLOCK):
            u_ref[l * rows:(l + 1) * rows, :] = z_ref[:, l, :]

    def load_u(first, rsl):
        return u_ref[rsl, :] if natural else z_ref[rsl, :]

    def store_y(first, rsl, y):
        if natural:
            for k, l in enumerate(range(first, first + offsets_per_chunk)):
                o_ref[:, l, :] = y[k * rows:(k + 1) * rows, :]
        else:
            o_ref[rsl, :] = y

    for first, rsl in row_chunks:
        xs_ref[rsl, :] = jnp.dot(load_u(first, rsl).astype(BF16), b_ref[0], preferred_element_type=F32)
        for l in range(max(first, 1), first + offsets_per_chunk):
            prev = slice((l - 1) * rows, l * rows)
            cur = slice(l * rows, (l + 1) * rows)
            for re, im in lane_chunks():
                lr = pw_ref[0, 1:2, re]
                li = pw_ref[0, 1:2, im]
                pr = xs_ref[prev, re]
                pi = xs_ref[prev, im]
                xs_ref[cur, re] += pr * lr - pi * li
                xs_ref[cur, im] += pr * li + pi * lr

    l8r = pw_ref[0, SSM_BLOCK:SSM_BLOCK + 1, 0:half]
    l8i = pw_ref[0, SSM_BLOCK:SSM_BLOCK + 1, half:2 * half]
    last = (SSM_BLOCK - 1) * rows

    def step(c, carry):
        hr, hi = carry
        r0 = c * nseq
        hin_ref[pl.ds(r0, nseq), 0:half] = hr
        hin_ref[pl.ds(r0, nseq), half:2 * half] = hi
        sr = xs_ref[pl.ds(last + r0, nseq), 0:half]
        si = xs_ref[pl.ds(last + r0, nseq), half:2 * half]
        return l8r * hr - l8i * hi + sr, l8r * hi + l8i * hr + si

    hr, hi = lax.fori_loop(0, nb, step, (h_ref[:, 0:half], h_ref[:, half:2 * half]), unroll=4)
    h_ref[:, 0:half] = hr
    h_ref[:, half:2 * half] = hi

    for first, rsl in row_chunks:
        for l in range(first, first + offsets_per_chunk):
            cur = slice(l * rows, (l + 1) * rows)
            for re, im in lane_chunks():
                lr = pw_ref[0, l + 1:l + 2, re]
                li = pw_ref[0, l + 1:l + 2, im]
                pr = hin_ref[:, re]
                pi = hin_ref[:, im]
                xs_ref[cur, re] += pr * lr - pi * li
                xs_ref[cur, im] += pr * li + pi * lr
        y = jnp.dot(xs_ref[rsl, :].astype(BF16), c_ref[0], preferred_element_type=F32)
        store_y(first, rsl, _gelu(y + d_ref[0] * load_u(first, rsl)))

    @pl.when(i == pl.num_programs(2) - 1)
    def _():
        hl_ref[0, 0] = h_ref[...]


def _ssm_mixer(z, bbd, cbd, pows, dsk, h0, *, ngroups, nseq, nb, name):
    slabs, sw, two_half = bbd.shape
    half = two_half // 2
    ssm_w = slabs * sw
    t = z.shape[0]
    tt = SSM_BLOCK * nb * nseq
    nt = t // (ngroups * tt)
    npow = pows.shape[1]
    natural = nseq == 1
    if natural:
        first_slab = (z.shape[1] - ssm_w) // sw
        z = z.reshape(t // SSM_BLOCK, SSM_BLOCK, z.shape[1])
        z_spec = pl.BlockSpec((nb, SSM_BLOCK, sw), lambda s, b, i: (b * nt + i, 0, first_slab + s))
        o_spec = pl.BlockSpec((nb, SSM_BLOCK, sw), lambda s, b, i: (b * nt + i, 0, s))
        o_shape = jax.ShapeDtypeStruct((t // SSM_BLOCK, SSM_BLOCK, ssm_w), F32)
    else:
        z_spec = o_spec = pl.BlockSpec((tt, sw), lambda s, b, i: (b * nt + i, s))
        o_shape = jax.ShapeDtypeStruct((t, ssm_w), F32)
    kernel = functools.partial(_ssm_kernel, nseq=nseq, nb=nb, half=half, chunk=min(half, 2 * LANES),
                               natural=natural)
    y, h_last = pl.pallas_call(
        kernel,
        grid=(slabs, ngroups, nt),
        in_specs=[
            z_spec,
            pl.BlockSpec((1, sw, 2 * half), lambda s, b, i: (s, 0, 0)),
            pl.BlockSpec((1, 2 * half, sw), lambda s, b, i: (s, 0, 0)),
            pl.BlockSpec((1, npow, 2 * half), lambda s, b, i: (s, 0, 0)),
            pl.BlockSpec((1, 1, sw), lambda s, b, i: (s, 0, 0)),
            pl.BlockSpec((1, 1, nseq, 2 * half), lambda s, b, i: (s, b, 0, 0)),
        ],
        out_specs=[
            o_spec,
            pl.BlockSpec((1, 1, nseq, 2 * half), lambda s, b, i: (s, b, 0, 0)),
        ],
        out_shape=[
            o_shape,
            jax.ShapeDtypeStruct((slabs, ngroups, nseq, 2 * half), F32),
        ],
        scratch_shapes=[
            pltpu.VMEM((tt, 2 * half), F32),
            pltpu.VMEM((nb * nseq, 2 * half), F32),
            pltpu.VMEM((nseq, 2 * half), F32),
            pltpu.VMEM((tt, sw) if natural else (SUBLANES, LANES), F32),
        ],
        compiler_params=_cparams("parallel", "arbitrary", "arbitrary"),
        name=name,
    )(z, bbd, cbd, pows, dsk, h0)
    return y.reshape(t, ssm_w), h_last


def _glu_kernel(s_ref, w_ref, b_ref, o_ref):
    s = s_ref[...]
    acc = jnp.dot(s.astype(BF16), w_ref[...], preferred_element_type=F32) + b_ref[...]
    o_ref[...] = (s * jax.nn.sigmoid(acc)).astype(o_ref.dtype)


def _glu(s, w, b, *, bm, name):
    t, k = s.shape
    n = w.shape[1]
    return pl.pallas_call(
        _glu_kernel,
        grid=(t // bm,),
        in_specs=[_rows(bm, k), _resident((k, n)), _resident((1, n))],
        out_specs=_rows(bm, n),
        out_shape=jax.ShapeDtypeStruct((t, n), BF16),
        compiler_params=_cparams("parallel"),
        name=name,
    )(s, w, b)


def _outproj_kernel(a_ref, b_ref, x_ref, w_ref, o_ref):
    lhs = jnp.concatenate([a_ref[...], b_ref[...]], axis=1)
    o_ref[...] = x_ref[...] + jnp.dot(lhs, w_ref[...], preferred_element_type=F32)


def _outproj(a, b, x, w, *, bm, name):
    t, ka = a.shape
    kb = b.shape[1]
    n = w.shape[1]
    return pl.pallas_call(
        _outproj_kernel,
        grid=(t // bm,),
        in_specs=[_rows(bm, ka), _rows(bm, kb), _rows(bm, n), _resident((ka + kb, n))],
        out_specs=_rows(bm, n),
        out_shape=jax.ShapeDtypeStruct((t, n), F32),
        compiler_params=_cparams("parallel"),
        name=name,
    )(a, b, x, w)


ROUTE_RANKS = PEER_TOPK + 1
ROUTE_PAIR_COUNTS = tuple(ROUTE_RANKS // (i + 1) for i in range(ROUTE_RANKS))


def _top_rows(x, k, with_rank=False):
    tops = []
    rank = jnp.full(x.shape, float(k), F32) if with_rank else None
    for i in range(k):
        m = jnp.max(x, axis=0, keepdims=True)
        tops.append(m)
        hit = x == m
        if with_rank:
            rank = jnp.where(hit, float(i), rank)
        x = jnp.where(hit, -jnp.inf, x)
    return (tops, rank) if with_rank else tops


def _route_kernel(q_ref, k_ref, ra_ref, rb_ref):
    for h in range(PEER_HEADS):
        def scores(p):
            c0 = (2 * h + p) * PEER_DK_HALF
            return lax.dot_general(k_ref[h, p], q_ref[:, c0:c0 + PEER_DK_HALF], (((1,), (1,)), ((), ())),
                                   precision=lax.Precision.HIGHEST, preferred_element_type=F32)

        s1, s2 = scores(0), scores(1)
        tops1 = _top_rows(s1, ROUTE_RANKS)
        tops2, rank2 = _top_rows(s2, ROUTE_RANKS, with_rank=True)
        v2 = jnp.concatenate(tops2, axis=0)
        cand = jnp.concatenate([tops1[i] + v2[0:n] for i, n in enumerate(ROUTE_PAIR_COUNTS)], axis=0)
        best = _top_rows(cand, ROUTE_RANKS)
        zsum = jnp.zeros_like(best[0])
        for b in best[:PEER_TOPK]:
            zsum = zsum + jnp.exp(b - best[0])
        thr = 0.5 * (best[PEER_TOPK - 1] + best[PEER_TOPK])
        need = thr - s1
        count = jnp.zeros_like(s1)
        for t2 in tops2[:PEER_TOPK // 2]:
            count = count + jnp.where(t2 >= need, 1.0, 0.0)
        need_best = thr - tops1[0]
        count_best = jnp.zeros_like(need_best)
        for t2 in tops2:
            count_best = count_best + jnp.where(t2 >= need_best, 1.0, 0.0)
        ra_ref[h, 0] = jnp.where(s1 == tops1[0], count_best, count)
        ra_ref[h, 1] = jnp.exp(s1 - tops1[0]) / zsum
        rb_ref[h, 0] = rank2.astype(BF16)
        rb_ref[h, 1] = jnp.exp(s2 - tops2[0]).astype(BF16)


def _route(q, keys, *, tt, name):
    t, dq = q.shape
    spec = pl.BlockSpec((PEER_HEADS, 2, PEER_NKEYS, tt), lambda i: (0, 0, 0, i))
    return pl.pallas_call(
        _route_kernel,
        grid=(t // tt,),
        in_specs=[
            pl.BlockSpec((tt, dq), lambda i: (i, 0)),
            pl.BlockSpec(keys.shape, lambda i: (0, 0, 0, 0)),
        ],
        out_specs=[spec, spec],
        out_shape=[
            jax.ShapeDtypeStruct((PEER_HEADS, 2, PEER_NKEYS, t), F32),
            jax.ShapeDtypeStruct((PEER_HEADS, 2, PEER_NKEYS, t), BF16),
        ],
        compiler_params=_cparams("parallel"),
        name=name,
    )(q, keys)


PEER_SUB = 2 * PEER_NKEYS


def _peer_kernel(xnt_ref, u_ref, vt_ref, ra_ref, rb_ref, o_ref, *, et):
    @pl.when(pl.program_id(1) == 0)
    def _():
        o_ref[...] = jnp.zeros_like(o_ref)

    tt = o_ref.shape[1]
    pack = 2 * SUBLANES
    ws = []
    for k in range(et // PEER_SUB):
        s_t = jnp.dot(u_ref[k * PEER_SUB:(k + 1) * PEER_SUB, :], xnt_ref[...], preferred_element_type=F32)
        gates = []
        for aa in range(PEER_SUB // PEER_NKEYS):
            a = k * (PEER_SUB // PEER_NKEYS) + aa
            acc = [None] * (PEER_NKEYS // pack)
            for h in range(PEER_HEADS):
                count = jnp.broadcast_to(ra_ref[h, 0, a:a + 1, :], (pack, tt)).astype(BF16)
                e1 = jnp.broadcast_to(ra_ref[h, 1, a:a + 1, :], (pack, tt)).astype(BF16)
                for r in range(PEER_NKEYS // pack):
                    rows = slice(r * pack, (r + 1) * pack)
                    term = jnp.where(rb_ref[h, 0, rows, :] < count, rb_ref[h, 1, rows, :], 0.0) * e1
                    acc[r] = term if acc[r] is None else acc[r] + term
            gates.extend(acc)
        ws.append(_gelu(s_t).astype(BF16) * jnp.concatenate(gates, axis=0))
    o_ref[...] += jnp.dot(vt_ref[0], jnp.concatenate(ws, axis=0), preferred_element_type=F32)


def _peer(xnt, u, vt, ra, rb, *, tt, name):
    d, t = xnt.shape
    nj, _, et = vt.shape
    once = pl.Buffered(1)
    return pl.pallas_call(
        functools.partial(_peer_kernel, et=et),
        grid=(t // tt, nj),
        in_specs=[
            pl.BlockSpec((d, tt), lambda i, j: (0, i)),
            pl.BlockSpec((et, d), lambda i, j: (j, 0)),
            pl.BlockSpec((1, d, et), lambda i, j: (j, 0, 0)),
            pl.BlockSpec((PEER_HEADS, 2, et // PEER_NKEYS, tt), lambda i, j: (0, 0, j, i)),
            pl.BlockSpec((PEER_HEADS, 2, PEER_NKEYS, tt), lambda i, j: (0, 0, 0, i)),
        ],
        out_specs=pl.BlockSpec((d, tt), lambda i, j: (0, i), pipeline_mode=once),
        out_shape=jax.ShapeDtypeStruct((d, t), F32),
        compiler_params=_cparams("parallel", "arbitrary"),
        name=name,
    )(xnt, u, vt, ra, rb)


def _ple_kernel(x1_ref, pt_ref, p_ref, g_ref, wg_ref, wp_ref, gf_ref, o_ref):
    x2 = x1_ref[...] + pt_ref[...].T
    inv = lax.rsqrt(jnp.mean(x2 * x2, axis=-1, keepdims=True) + EPS)
    pre = jnp.dot((x2 * g_ref[...]).astype(BF16), wg_ref[...], preferred_element_type=F32)
    gate = jax.nn.sigmoid(inv * pre)
    ple = jnp.dot(p_ref[...].astype(BF16), wp_ref[...], preferred_element_type=F32)
    o_ref[...] = _rmsnorm(x2 + ple * gate, gf_ref[...])


def _ple(x1, peer_t, p, g_ple, w_gate, w_ple, g_final, *, bm, name):
    t, d = x1.shape
    dp = p.shape[1]
    return pl.pallas_call(
        _ple_kernel,
        grid=(t // bm,),
        in_specs=[
            _rows(bm, d),
            pl.BlockSpec((d, bm), lambda i: (0, i)),
            _rows(bm, dp),
            _resident((1, d)),
            _resident((d, d)),
            _resident((dp, d)),
            _resident((1, d)),
        ],
        out_specs=_rows(bm, d),
        out_shape=jax.ShapeDtypeStruct((t, d), F32),
        compiler_params=_cparams("parallel"),
        name=name,
    )(x1, peer_t, p, g_ple, w_gate, w_ple, g_final)


def _tile(n, pref):
    return pref if n % pref == 0 else n


def _stream(x, p, pool_init, h0_re, h0_im, pos0, wts, *, tag):
    bsz, seq, d = x.shape
    t = bsz * seq
    x2d = x.reshape(t, d)
    mix_w = wts["w_in"].shape[1]
    pool_w = mix_w // 2
    ssm_w = mix_w - pool_w
    slabs, sw, two_half = wts["bbd"].shape
    half = two_half // 2

    z = _norm_matmul(x2d, wts["g_mix"], wts["w_in"], bm=_tile(t, PROJ_ROWS), emit_xn=False, name=f"in_proj_{tag}")

    y_pool = _pool_mixer(z, pool_init, wts["w_pool"], wts["pool_scale"], nseq=bsz, seq_len=seq,
                         tt=_tile(seq, POOL_ROWS), pos0=pos0, name=f"pool_{tag}")

    long_seq = seq >= SSM_TILE_TOKENS
    if long_seq:
        nb, nseq, ngroups = SSM_TILE_TOKENS // SSM_BLOCK, 1, bsz
        zp = z
    else:
        nb, nseq, ngroups = seq // SSM_BLOCK, bsz, 1
        zp = z[:, pool_w:].reshape(bsz, nb, SSM_BLOCK, ssm_w).transpose(2, 1, 0, 3).reshape(t, ssm_w)

    def to_slabs(h):
        return h.reshape(ngroups, nseq, slabs, half).transpose(2, 0, 1, 3)

    h0 = jnp.concatenate([to_slabs(h0_re), to_slabs(h0_im)], axis=-1)
    sp, h_last = _ssm_mixer(zp, wts["bbd"], wts["cbd"], wts["pows"], wts["dsk"], h0,
                            ngroups=ngroups, nseq=nseq, nb=nb, name=f"ssm_{tag}")
    sp = _glu(sp, wts["w_glu"], wts["b_glu"], bm=_tile(t, GLU_ROWS), name=f"glu_{tag}")
    s = sp if long_seq else sp.reshape(SSM_BLOCK, nb, bsz, ssm_w).transpose(2, 1, 0, 3).reshape(t, ssm_w)

    def from_slabs(h):
        return h.transpose(1, 2, 0, 3).reshape(bsz, slabs * half // SSM_N, SSM_N)

    new_re = from_slabs(h_last[..., :half])
    new_im = from_slabs(h_last[..., half:])

    x1 = _outproj(y_pool, s, x2d, wts["w_out"], bm=_tile(t, PROJ_ROWS), name=f"out_proj_{tag}")

    q, xnt = _norm_matmul(x1, wts["g_ffn"], wts["w_query"], bm=_tile(t, PROJ_ROWS), emit_xn=True, name=f"query_{tag}")
    tt = _tile(t, PEER_TOKEN_TILE)
    ra, rb = _route(q, wts["keys"], tt=tt, name=f"route_{tag}")
    peer_t = _peer(xnt, wts["expert_u"], wts["expert_vt"], ra, rb, tt=tt, name=f"peer_{tag}")

    y = _ple(x1, peer_t, p.reshape(t, -1), wts["g_ple"], wts["w_ple_gate"], wts["w_ple"], wts["g_final"],
             bm=_tile(t, PLE_ROWS), name=f"ple_{tag}")

    new_pool = z.reshape(bsz, seq, mix_w)[:, seq - (POOL_HALO - 1):, :pool_w]
    return y.reshape(bsz, seq, d), new_pool, new_re, new_im


def _layer_weights(i, g_mix, w_in, w_pool, pool_scale, ssm_a_re, ssm_a_im, ssm_log_dt, ssm_b_re, ssm_b_im,
                   ssm_c_re, ssm_c_im, ssm_d, w_glu, b_glu, w_out, g_ffn, w_query, peer_sub_keys,
                   expert_u, expert_v, g_ple, w_ple_gate, w_ple, g_final):
    groups, n = ssm_a_re[i].shape
    slabs = groups // SSM_SLAB_GROUPS
    pre, pim, bre, bim = _ssm_params(ssm_a_re[i], ssm_a_im[i], ssm_log_dt[i], ssm_b_re[i], ssm_b_im[i])

    def b_blocks(b):
        return b.reshape(SSM_CH, slabs, SSM_SLAB_GROUPS, n).transpose(1, 2, 0, 3)

    bbd = jnp.concatenate([_block_diag(b_blocks(bre)), _block_diag(b_blocks(bim))], axis=-1).astype(BF16)

    def c_blocks(c):
        return c.reshape(slabs, SSM_SLAB_GROUPS, SSM_CH, n).transpose(0, 1, 3, 2)

    cbd = jnp.concatenate([_block_diag(c_blocks(ssm_c_re[i])), _block_diag(c_blocks(-ssm_c_im[i]))],
                          axis=1).astype(BF16)

    def pow_rows(pw):
        return pw.reshape(pw.shape[0], slabs, SSM_SLAB_GROUPS * n).transpose(1, 0, 2)

    pows = jnp.concatenate([pow_rows(pre), pow_rows(pim)], axis=-1)
    pows = jnp.pad(pows, ((0, 0), (0, 2 * SUBLANES - pows.shape[1]), (0, 0)))
    dsk = ssm_d[i].reshape(slabs, 1, SSM_SLAB_GROUPS * SSM_CH)

    ne, d = expert_v[i].shape
    expert_vt = expert_v[i].reshape(ne // PEER_EXPERT_TILE, PEER_EXPERT_TILE, d).transpose(0, 2, 1).astype(BF16)

    row = lambda v: v.reshape(1, -1)
    return dict(
        g_mix=row(g_mix[i]), w_in=w_in[i].astype(BF16), w_pool=w_pool[i].astype(BF16),
        pool_scale=row(pool_scale[i]), bbd=bbd, cbd=cbd, pows=pows, dsk=dsk,
        w_glu=w_glu[i].astype(BF16), b_glu=row(b_glu[i]), w_out=w_out[i].astype(BF16),
        g_ffn=row(g_ffn[i]), w_query=w_query[i].astype(BF16), keys=peer_sub_keys[i],
        expert_u=expert_u[i].astype(BF16), expert_vt=expert_vt,
        g_ple=row(g_ple[i]), w_ple_gate=w_ple_gate[i].astype(BF16), w_ple=w_ple[i].astype(BF16),
        g_final=row(g_final),
    )


def kernel(x_prompt, x_sample, p_prompt, p_sample, cache_pool, state_ssm_re, state_ssm_im, g_mix, w_in, w_pool, pool_scale, ssm_a_re, ssm_a_im, ssm_log_dt, ssm_b_re, ssm_b_im, ssm_c_re, ssm_c_im, ssm_d, w_glu, b_glu, w_out, g_ffn, w_query, peer_sub_keys, expert_u, expert_v, g_ple, w_ple_gate, w_ple, g_final):
    depth = g_mix.shape[0]
    assert depth == 1, "the final rmsnorm is fused into the layer's last kernel"
    bp = x_prompt.shape[0]
    groups, n = ssm_a_re.shape[1:]
    pool_w = cache_pool.shape[-1]

    wts = _layer_weights(0, g_mix, w_in, w_pool, pool_scale, ssm_a_re, ssm_a_im, ssm_log_dt, ssm_b_re, ssm_b_im,
                         ssm_c_re, ssm_c_im, ssm_d, w_glu, b_glu, w_out, g_ffn, w_query, peer_sub_keys,
                         expert_u, expert_v, g_ple, w_ple_gate, w_ple, g_final)

    zero_pool = jnp.zeros((bp, POOL_HALO, pool_w), F32)
    zero_h = jnp.zeros((bp, groups, n), F32)
    yp, pool_p, re_p, im_p = _stream(x_prompt, p_prompt[0], zero_pool, zero_h, zero_h, 0, wts, tag="prompt")
    pool_init = jnp.pad(cache_pool[0], ((0, 0), (1, 0), (0, 0)))
    ys, pool_s, re_s, im_s = _stream(x_sample, p_sample[0], pool_init, state_ssm_re[0], state_ssm_im[0],
                                     SAMPLE_PAST_LEN, wts, tag="sample")
    return (yp, ys, pool_p[None], re_p[None], im_p[None], pool_s[None], re_s[None], im_s[None])
```
